```python
import jax, jax.numpy as jnp
from jax import lax
import numpy as np

D_MODEL = 4096
BATCH = 4
SEQ = 4096
DEPTH = 2
DEC_BATCH = 16
DEC_SEQ = 64
PAST_LEN = 4096

CHUNK = 64
MIX_WIDTH = D_MODEL
M_WIDTH = MIX_WIDTH // 2
H_WIDTH = MIX_WIDTH - M_WIDTH
M_HEADS = 4
M_DV = M_WIDTH // M_HEADS
M_DK = M_DV // 2
M_QK = M_HEADS * M_DK
H_DK = 128
H_HEADS = H_WIDTH // H_DK
H_DV = H_WIDTH // H_HEADS
D_FF = 4 * D_MODEL
HG_BLOCK = 16
EPS = 1e-6
NEG_BIG = -1e30
LB_FLOOR = 1e-30
IN_SIZES = (M_QK, M_QK, M_WIDTH, M_WIDTH, M_HEADS, M_HEADS, H_WIDTH, H_WIDTH, H_WIDTH, H_WIDTH)
IN_COLS = sum(IN_SIZES)

kernel_name = "hymba_mlstm_hgrn2_streaming_step"


def _rmsnorm(x, g):
    xf = x.astype(jnp.float32)
    y = xf * lax.rsqrt(jnp.mean(xf * xf, axis=-1, keepdims=True) + EPS)
    return (y * g.astype(jnp.float32)).astype(x.dtype)


def _block_len(T, pref):
    return max(d for d in range(1, min(pref, T) + 1) if T % d == 0)


def _to_blocks(a, L):
    B, T, H = a.shape[:3]
    return a.reshape(B, T // L, L, H, -1).transpose(1, 0, 3, 2, 4)


def _from_blocks(a):
    NC, B, H, L, d = a.shape
    return a.transpose(1, 0, 3, 2, 4).reshape(B, NC * L, H, d)


def _mlstm(q, k, v, ig, lf, C0, n0, m0):
    T = q.shape[1]
    L = _block_len(T, CHUNK)
    causal = jnp.tril(jnp.ones((L, L), dtype=bool))
    xs = (_to_blocks(q, L), _to_blocks(k, L), _to_blocks(v, L),
          _to_blocks(ig[..., None], L)[..., 0], _to_blocks(lf[..., None], L)[..., 0])

    def step(carry, blk):
        C, n, m = carry
        qc, kc, vc, ic, fc = blk
        b = jnp.cumsum(fc, axis=-1)
        inter = b + m[..., None]
        D = jnp.where(causal, b[..., :, None] - b[..., None, :] + ic[..., None, :], NEG_BIG)
        mt = jnp.maximum(inter, jnp.max(D, axis=-1))
        a_inter = jnp.exp(inter - mt)
        S = jnp.einsum('bhtd,bhsd->bhts', qc, kc) * jnp.exp(D - mt[..., None])
        num = a_inter[..., None] * jnp.einsum('bhtd,bhde->bhte', qc, C) + jnp.einsum('bhts,bhse->bhte', S, vc)
        den = a_inter * jnp.einsum('bhtd,bhd->bht', qc, n) + jnp.sum(S, axis=-1)
        h = num / jnp.maximum(jnp.abs(den), jnp.exp(-mt))[..., None]
        wlast = b[..., -1:] - b + ic
        m_new = jnp.maximum(b[..., -1] + m, jnp.max(wlast, axis=-1))
        a_c = jnp.exp(b[..., -1] + m - m_new)
        ws = jnp.exp(wlast - m_new[..., None])
        C_new = a_c[..., None, None] * C + jnp.einsum('bhs,bhsd,bhse->bhde', ws, kc, vc)
        n_new = a_c[..., None] * n + jnp.einsum('bhs,bhsd->bhd', ws, kc)
        return (C_new, n_new, m_new), h

    (C, n, m), hs = lax.scan(step, (C0, n0, m0), xs)
    return _from_blocks(hs), C, n, m


def _hgrn2(q, kk, g, v, S0):
    T = q.shape[1]
    L = _block_len(T, HG_BLOCK)
    causal = jnp.tril(jnp.ones((L, L), dtype=bool))[:, :, None]
    xs = (_to_blocks(q, L), _to_blocks(kk, L), _to_blocks(g, L), _to_blocks(v, L))

    def step(S, blk):
        qc, kc, gc, vc = blk
        G = jnp.cumsum(gc, axis=2)
        o_inter = jnp.einsum('bhtd,bhde->bhte', qc * jnp.exp(G), S)
        diff = G[:, :, :, None, :] - G[:, :, None, :, :]
        decay = jnp.where(causal, jnp.exp(jnp.where(causal, diff, 0.0)), 0.0)
        A = jnp.einsum('bhtd,bhsd,bhtsd->bhts', qc, kc, decay)
        o = o_inter + jnp.einsum('bhts,bhse->bhte', A, vc)
        Gl = G[:, :, -1]
        S_new = jnp.exp(Gl)[..., None] * S + jnp.einsum('bhsd,bhse->bhde', kc * jnp.exp(Gl[:, :, None] - G), vc)
        return S_new, o

    S, os_ = lax.scan(step, S0, xs)
    return _from_blocks(os_), S


def _mixer(h, w_in, b_gate, lb, m_g, h_g, w_out, C0, n0, m0, S0):
    B, T, _ = h.shape
    z = h @ w_in
    mq, mk, mv, mo, mi, mf, hq, hf, hi, hg = jnp.split(z, np.cumsum(IN_SIZES)[:-1].tolist(), axis=-1)
    f32 = jnp.float32
    q = mq.astype(f32).reshape(B, T, M_HEADS, M_DK) * (M_DK ** -0.5)
    k = mk.astype(f32).reshape(B, T, M_HEADS, M_DK)
    v = mv.astype(f32).reshape(B, T, M_HEADS, M_DV)
    bg = b_gate.astype(f32)
    ig = mi.astype(f32) + bg[:M_HEADS]
    lf = jax.nn.log_sigmoid(mf.astype(f32) + bg[M_HEADS:])
    hm, C, n, m = _mlstm(q, k, v, ig, lf, C0.astype(f32), n0.astype(f32), m0.astype(f32))
    hm = _rmsnorm(hm, m_g.reshape(M_HEADS, M_DV)).reshape(B, T, M_WIDTH) * jax.nn.sigmoid(mo.astype(f32))
    fx = hf.astype(f32).reshape(B, T, H_HEADS, H_DK)
    lbr = lb.reshape(H_HEADS, H_DK)
    log_lb = jnp.log(jnp.maximum(lbr, LB_FLOOR))
    g = jnp.logaddexp(log_lb, jnp.log1p(-lbr) + jax.nn.log_sigmoid(fx))
    kk = (1.0 - lbr) * jax.nn.sigmoid(-fx)
    qh = jax.nn.silu(hq.astype(f32)).reshape(B, T, H_HEADS, H_DK)
    vh = hi.astype(f32).reshape(B, T, H_HEADS, H_DV)
    oh, S = _hgrn2(qh, kk, g, vh, S0.astype(f32))
    oh = _rmsnorm(oh, h_g.reshape(H_HEADS, H_DV)).reshape(B, T, H_WIDTH) * jax.nn.silu(hg.astype(f32))
    out = jnp.concatenate([hm, oh], axis=-1).astype(h.dtype) @ w_out
    return out, C, n, m, S


def _trunk(x, c, C0, n0, m0, S0, w_mod, b_mod, norm1_g, w_in, b_gate, lower_bounds,
           mlstm_norm_g, hgrn_norm_g, w_out, norm2_g, w_up, w_down, final_g):
    sm = jax.nn.softmax(lower_bounds.astype(jnp.float32), axis=0)
    lbs = jnp.cumsum(sm, axis=0) - sm[0:1]
    Cs, ns, ms, Ss = [], [], [], []
    for l in range(DEPTH):
        mod = (jax.nn.silu(c) @ w_mod[l] + b_mod[l])[:, None, :]
        sh1, sc1, ga1, sh2, sc2, ga2 = jnp.split(mod, 6, axis=-1)
        h = _rmsnorm(x, norm1_g[l]) * (1 + sc1) + sh1
        mix, C, n, m, S = _mixer(h, w_in[l], b_gate[l], lbs[l], mlstm_norm_g[l], hgrn_norm_g[l], w_out[l],
                                 C0[l], n0[l], m0[l], S0[l])
        x = x + ga1 * mix
        h = _rmsnorm(x, norm2_g[l]) * (1 + sc2) + sh2
        x = x + ga2 * (jnp.square(jax.nn.relu(h @ w_up[l])) @ w_down[l])
        Cs.append(C); ns.append(n); ms.append(m); Ss.append(S)
    y = _rmsnorm(x, final_g)
    return y, jnp.stack(Cs), jnp.stack(ns), jnp.stack(ms), jnp.stack(Ss)


def setup_inputs(seed: int = 0) -> dict:
    key = jax.random.key(seed)
    ks = jax.random.split(key, 24)
    nrm = jax.random.normal
    f32 = jnp.float32
    return {
        "x_prompt": nrm(ks[0], (BATCH, SEQ, D_MODEL), f32),
        "x_sample": nrm(ks[1], (DEC_BATCH, DEC_SEQ, D_MODEL), f32),
        "state_mlstm_C": 0.05 * nrm(ks[2], (DEPTH, DEC_BATCH, M_HEADS, M_DK, M_DV), f32),
        "state_mlstm_n": 0.5 * nrm(ks[3], (DEPTH, DEC_BATCH, M_HEADS, M_DK), f32),
        "state_mlstm_m": 0.5 * nrm(ks[4], (DEPTH, DEC_BATCH, M_HEADS), f32),
        "state_hgrn_S": 0.5 * nrm(ks[5], (DEPTH, DEC_BATCH, H_HEADS, H_DK, H_DV), f32),
        "c_prompt": nrm(ks[6], (BATCH, D_MODEL), f32),
        "c_sample": nrm(ks[7], (DEC_BATCH, D_MODEL), f32),
        "w_mod": 0.5 * D_MODEL ** -0.5 * nrm(ks[8], (DEPTH, D_MODEL, 6 * D_MODEL), f32),
        "b_mod": 0.02 * nrm(ks[9], (DEPTH, 6 * D_MODEL), f32),
        "norm1_g": 1.0 + 0.05 * nrm(ks[10], (DEPTH, D_MODEL), f32),
        "w_in": D_MODEL ** -0.5 * nrm(ks[11], (DEPTH, D_MODEL, IN_COLS), f32),
        "b_gate": jnp.concatenate([-1.0 + 0.1 * nrm(ks[12], (DEPTH, M_HEADS), f32),
                                   3.0 + 0.5 * nrm(ks[13], (DEPTH, M_HEADS), f32)], axis=-1),
        "lower_bounds": 0.5 * nrm(ks[14], (DEPTH, H_WIDTH), f32),
        "mlstm_norm_g": 1.0 + 0.05 * nrm(ks[15], (DEPTH, M_WIDTH), f32),
        "hgrn_norm_g": 1.0 + 0.05 * nrm(ks[16], (DEPTH, H_WIDTH), f32),
        "w_out": MIX_WIDTH ** -0.5 * nrm(ks[17], (DEPTH, MIX_WIDTH, D_MODEL), f32),
        "norm2_g": 1.0 + 0.05 * nrm(ks[18], (DEPTH, D_MODEL), f32),
        "w_up": D_MODEL ** -0.5 * nrm(ks[19], (DEPTH, D_MODEL, D_FF), f32),
        "w_down": D_FF ** -0.5 * nrm(ks[20], (DEPTH, D_FF, D_MODEL), f32),
        "final_g": 1.0 + 0.05 * nrm(ks[21], (D_MODEL,), f32),
    }


def reference(x_prompt, x_sample, state_mlstm_C, state_mlstm_n, state_mlstm_m, state_hgrn_S,
              c_prompt, c_sample, w_mod, b_mod, norm1_g, w_in, b_gate, lower_bounds,
              mlstm_norm_g, hgrn_norm_g, w_out, norm2_g, w_up, w_down, final_g):
    weights = (w_mod, b_mod, norm1_g, w_in, b_gate, lower_bounds, mlstm_norm_g, hgrn_norm_g,
               w_out, norm2_g, w_up, w_down, final_g)
    Bp = x_prompt.shape[0]
    f32 = jnp.float32
    zC = jnp.zeros((DEPTH, Bp, M_HEADS, M_DK, M_DV), f32)
    zn = jnp.zeros((DEPTH, Bp, M_HEADS, M_DK), f32)
    zm = jnp.zeros((DEPTH, Bp, M_HEADS), f32)
    zS = jnp.zeros((DEPTH, Bp, H_HEADS, H_DK, H_DV), f32)
    y_prompt, pC, pn, pm, pS = _trunk(x_prompt, c_prompt, zC, zn, zm, zS, *weights)
    y_sample, sC, sn, sm, sS = _trunk(x_sample, c_sample, state_mlstm_C, state_mlstm_n, state_mlstm_m,
                                      state_hgrn_S, *weights)
    return (y_prompt, y_sample, pC, pn, pm, pS, sC, sn, sm, sS)
```

```python
import functools

import jax
import jax.numpy as jnp
from jax import lax
from jax.experimental import pallas as pl
from jax.experimental.pallas import tpu as pltpu

EPS = 1e-6
NEG_BIG = -1e30
LB_FLOOR = 1e-30
MLSTM_CHUNK = 64
HGRN_BLOCK = 16
HGRN_CHUNK = 64
LANES = 128
SUBLANES = 8
BF16_ROWS = 16
MIB = 1024 * 1024
VMEM_CAP = 56 * MIB

F32 = jnp.float32
BF16 = jnp.bfloat16


def _params(semantics, vmem_bytes):
    return pltpu.CompilerParams(dimension_semantics=semantics,
                                vmem_limit_bytes=int(min(VMEM_CAP, vmem_bytes)))


def _sigmoid(x):
    return 1.0 / (1.0 + jnp.exp(-x))


def _silu(x):
    return x * _sigmoid(x)


def _log_sigmoid(x):
    return jnp.minimum(x, 0.0) - jnp.log1p(jnp.exp(-jnp.abs(x)))


def _lower_tri(n, dtype):
    r = lax.broadcasted_iota(jnp.int32, (n, n), 0)
    c = lax.broadcasted_iota(jnp.int32, (n, n), 1)
    return jnp.where(r >= c, 1.0, 0.0).astype(dtype)


def _cumsum_rows(x, tri):
    hi = x.astype(BF16)
    r1 = x - hi.astype(F32)
    mid = r1.astype(BF16)
    lo = (r1 - mid.astype(F32)).astype(BF16)
    dot = functools.partial(jnp.dot, preferred_element_type=F32)
    return (dot(tri, hi) + dot(tri, mid)) + dot(tri, lo)


def _col_tile(n, pref):
    return max(t for t in range(LANES, min(pref, n) + 1, LANES) if n % t == 0)


def _row_groups(B, T, tm_pref):
    if T >= tm_pref:
        assert T % tm_pref == 0
        R, TG = tm_pref, 1
    else:
        R = T
        TG = max(1, min(B, tm_pref // T))
        assert B % TG == 0
    return (B * T) // R, R, TG, T // R


def _per_group(v, reps):
    if reps > 1:
        v = jnp.repeat(v, reps, axis=0)
    return v[:, None, :]


def _mod_norm(x, g, sc, sh):
    ms = jnp.mean(x * x, axis=-1, keepdims=True)
    return (x * lax.rsqrt(ms + EPS) * g) * (1.0 + sc) + sh


def _mod_kernel(c_ref, w_ref, b_ref, o_ref):
    c = c_ref[...]
    a = _silu(c).astype(BF16)
    w = w_ref[0].astype(BF16)
    o_ref[0] = jnp.dot(a, w, preferred_element_type=F32) + b_ref[0]


def _modulation(c_all, w_mod, b_mod, *, tn=512):
    depth, D, N = w_mod.shape
    MP = c_all.shape[0]
    tn = _col_tile(N, tn)
    vmem = 2 * (D * tn * 4 + MP * tn * 4 + tn * 4) + 2 * MP * D * 4 + D * tn * 2 + 4 * MIB
    return pl.pallas_call(
        _mod_kernel,
        grid=(depth, N // tn),
        in_specs=[
            pl.BlockSpec((MP, D), lambda l, j: (0, 0)),
            pl.BlockSpec((1, D, tn), lambda l, j: (l, 0, j)),
            pl.BlockSpec((1, 1, tn), lambda l, j: (l, 0, j)),
        ],
        out_specs=pl.BlockSpec((1, MP, tn), lambda l, j: (l, 0, j)),
        out_shape=jax.ShapeDtypeStruct((depth, MP, N), F32),
        compiler_params=_params(("parallel", "parallel"), vmem),
        name="modulation",
    )(c_all, w_mod, b_mod.reshape(depth, 1, N))


def _in_kernel(x_ref, sc_ref, sh_ref, g_ref, w_ref, wg_ref, z_ref, zg_ref, h_scr):
    @pl.when(pl.program_id(1) == 0)
    def _():
        h = _mod_norm(x_ref[...], g_ref[...], sc_ref[...], sh_ref[...])
        hb = h.reshape(h_scr.shape).astype(BF16)
        h_scr[...] = hb
        zg_ref[...] = jnp.dot(hb, wg_ref[...], preferred_element_type=F32)

    z_ref[...] = jnp.dot(h_scr[...], w_ref[...], preferred_element_type=F32)


def _in_proj(x, sc, sh, g, w_main, w_gate, *, tm=512, tn=1024):
    B, T, D = x.shape
    NZ = w_main.shape[1]
    G, R, TG, reps = _row_groups(B, T, tm)
    TM, M = TG * R, B * T
    tn = _col_tile(NZ, tn)
    vmem = 2 * (TM * D * 4 + D * tn * 2 + D * LANES * 2 + TM * tn * 4 + TM * LANES * 4
                + (2 * TG + 1) * SUBLANES * D * 4) + TM * D * 10 + 4 * MIB
    z, zg = pl.pallas_call(
        _in_kernel,
        grid=(G // TG, NZ // tn),
        in_specs=[
            pl.BlockSpec((TG, R, D), lambda i, j: (i, 0, 0)),
            pl.BlockSpec((TG, 1, D), lambda i, j: (i, 0, 0)),
            pl.BlockSpec((TG, 1, D), lambda i, j: (i, 0, 0)),
            pl.BlockSpec((1, 1, D), lambda i, j: (0, 0, 0)),
            pl.BlockSpec((D, tn), lambda i, j: (0, j)),
            pl.BlockSpec((D, LANES), lambda i, j: (0, 0)),
        ],
        out_specs=[
            pl.BlockSpec((TM, tn), lambda i, j: (i, j)),
            pl.BlockSpec((TM, LANES), lambda i, j: (i, 0)),
        ],
        out_shape=[jax.ShapeDtypeStruct((M, NZ), F32), jax.ShapeDtypeStruct((M, LANES), F32)],
        scratch_shapes=[pltpu.VMEM((TM, D), BF16)],
        compiler_params=_params(("parallel", "arbitrary"), vmem),
        name="in_proj",
    )(x.reshape(G, R, D), _per_group(sc, reps), _per_group(sh, reps), g.reshape(1, 1, D), w_main, w_gate)
    return z.reshape(B, T, NZ), zg.reshape(B, T, LANES)


def _mlstm_kernel(q_ref, k_ref, v_ref, o_ref, zg_ref, bg_ref, mg_ref, C0_ref, n0_ref, m0_ref,
                  hm_ref, C_ref, n_ref, m_ref, *, heads, dk, dv, chunk):
    L = chunk

    @pl.when(pl.program_id(1) == 0)
    def _():
        C_ref[...] = C0_ref[...]
        n_ref[...] = n0_ref[...]
        m_ref[...] = m0_ref[...]

    gates = zg_ref[0] + bg_ref[...]
    tri = _lower_tri(L, BF16)
    bsum = _cumsum_rows(_log_sigmoid(gates), tri)
    lane = lax.broadcasted_iota(jnp.int32, gates.shape, 1)
    rows = jnp.where(lane < heads, gates, bsum).T
    causal = (lax.broadcasted_iota(jnp.int32, (L, L), 0) >= lax.broadcasted_iota(jnp.int32, (L, L), 1))

    for h in range(heads):
        i_col = gates[:, h:h + 1]
        b_col = bsum[:, heads + h:heads + h + 1]
        i_row = rows[h:h + 1, :]
        b_row = rows[heads + h:heads + h + 1, :]
        m_prev = m_ref[0, h:h + 1, 0:1]
        n_prev = n_ref[0, h:h + 1, :]
        C_prev = C_ref[0, h]

        q = q_ref[0, :, h * dk:(h + 1) * dk] * (dk ** -0.5)
        k = k_ref[0, :, h * dk:(h + 1) * dk]
        vb = v_ref[0, :, h * dv:(h + 1) * dv].astype(BF16)
        qb = q.astype(BF16)

        D = jnp.where(causal, (b_col - b_row) + i_row, NEG_BIG)
        inter = b_col + m_prev
        mt = jnp.maximum(inter, jnp.max(D, axis=-1, keepdims=True))
        a_inter = jnp.exp(inter - mt)
        S = lax.dot_general(qb, k.astype(BF16), (((1,), (1,)), ((), ())),
                            preferred_element_type=F32) * jnp.exp(D - mt)
        num = a_inter * jnp.dot(qb, C_prev.astype(BF16), preferred_element_type=F32) \
            + jnp.dot(S.astype(BF16), vb, preferred_element_type=F32)
        den = a_inter * jnp.sum(q * n_prev, axis=-1, keepdims=True) + jnp.sum(S, axis=-1, keepdims=True)
        hh = num / jnp.maximum(jnp.abs(den), jnp.exp(-mt))

        b_last = b_col[L - 1:L, :]
        wlast = (b_last - b_col) + i_col
        m_new = jnp.maximum(b_last + m_prev, jnp.max(wlast, axis=0, keepdims=True))
        a_c = jnp.exp((b_last + m_prev) - m_new)
        kw = k * jnp.exp(wlast - m_new)
        C_ref[0, h] = a_c * C_prev + lax.dot_general(kw.astype(BF16), vb, (((0,), (0,)), ((), ())),
                                                     preferred_element_type=F32)
        n_ref[0, h:h + 1, :] = a_c * n_prev + jnp.sum(kw, axis=0, keepdims=True)
        m_ref[0, h:h + 1, :] = jnp.broadcast_to(m_new, (1, LANES))

        ms = jnp.mean(hh * hh, axis=-1, keepdims=True)
        y = (hh * lax.rsqrt(ms + EPS)) * mg_ref[:, h * dv:(h + 1) * dv]
        y = y * _sigmoid(o_ref[0, :, h * dv:(h + 1) * dv])
        hm_ref[0, :, h * dv:(h + 1) * dv] = y.astype(BF16)


def _mlstm(z, zg, b_gate, norm_g, C0, n0, m0):
    B, T, _ = z.shape
    _, heads, dk, dv = C0.shape
    QW, VW = heads * dk, heads * dv
    assert VW == 2 * QW and 2 * heads <= LANES
    L = MLSTM_CHUNK if T % MLSTM_CHUNK == 0 else T
    bias = jnp.zeros((1, LANES), F32).at[0, :2 * heads].set(b_gate)
    m0b = jnp.broadcast_to(m0[:, :, None], (B, heads, LANES))
    vmem = 2 * (L * (2 * QW + 2 * VW + LANES) * 4 + L * VW * 2) + 4 * heads * dk * dv * 4 + 12 * MIB
    kern = functools.partial(_mlstm_kernel, heads=heads, dk=dk, dv=dv, chunk=L)
    hm, C, n, m = pl.pallas_call(
        kern,
        grid=(B, T // L),
        in_specs=[
            pl.BlockSpec((1, L, QW), lambda b, c: (b, c, 0)),
            pl.BlockSpec((1, L, QW), lambda b, c: (b, c, 1)),
            pl.BlockSpec((1, L, VW), lambda b, c: (b, c, 1)),
            pl.BlockSpec((1, L, VW), lambda b, c: (b, c, 2)),
            pl.BlockSpec((1, L, LANES), lambda b, c: (b, c, 0)),
            pl.BlockSpec((1, LANES), lambda b, c: (0, 0)),
            pl.BlockSpec((1, VW), lambda b, c: (0, 0)),
            pl.BlockSpec((1, heads, dk, dv), lambda b, c: (b, 0, 0, 0)),
            pl.BlockSpec((1, heads, dk), lambda b, c: (b, 0, 0)),
            pl.BlockSpec((1, heads, LANES), lambda b, c: (b, 0, 0)),
        ],
        out_specs=[
            pl.BlockSpec((1, L, VW), lambda b, c: (b, c, 0)),
            pl.BlockSpec((1, heads, dk, dv), lambda b, c: (b, 0, 0, 0)),
            pl.BlockSpec((1, heads, dk), lambda b, c: (b, 0, 0)),
            pl.BlockSpec((1, heads, LANES), lambda b, c: (b, 0, 0)),
        ],
        out_shape=[
            jax.ShapeDtypeStruct((B, T, VW), BF16),
            jax.ShapeDtypeStruct((B, heads, dk, dv), F32),
            jax.ShapeDtypeStruct((B, heads, dk), F32),
            jax.ShapeDtypeStruct((B, heads, LANES), F32),
        ],
        compiler_params=_params(("parallel", "arbitrary"), vmem),
        name="mlstm",
    )(z, z, z, z, zg, bias, norm_g.reshape(1, VW), C0, n0, m0b)
    return hm, C, n, m[:, :, 0]


def _hgrn_kernel(q_ref, f_ref, i_ref, g_ref, lb_ref, ng_ref, S0_ref, oh_ref, S_ref, st_scr,
                 *, heads, dk, dv, layer, chunk):
    nb = HGRN_BLOCK
    half = SUBLANES
    c = pl.program_id(1)

    @pl.when(c == 0)
    def _():
        for h in range(heads):
            st_scr[h] = S0_ref[0, h].T

    lbs = lb_ref[...]
    e = jnp.exp(lbs - jnp.max(lbs, axis=0, keepdims=True))
    sm = e / jnp.sum(e, axis=0, keepdims=True)
    cum = sm[0:1, :]
    for j in range(1, layer + 1):
        cum = cum + sm[j:j + 1, :]
    lb = cum - sm[0:1, :]
    log_lb = jnp.log(jnp.maximum(lb, LB_FLOOR))
    l1m = jnp.log1p(-lb)
    one_m = 1.0 - lb

    tri = _lower_tri(nb, BF16)
    row8 = lax.broadcasted_iota(jnp.int32, (half, 1), 0)

    def block(j, carry):
        rows = pl.ds(pl.multiple_of(j * nb, nb), nb)
        fx = f_ref[0, rows, :]
        bb = l1m + _log_sigmoid(fx)
        g = jnp.maximum(log_lb, bb) + jnp.log1p(jnp.exp(-jnp.abs(log_lb - bb)))
        kk = one_m * _sigmoid(-fx)
        qs = _silu(q_ref[0, rows, :])
        v = i_ref[0, rows, :]
        G = _cumsum_rows(g, tri)
        gate = _silu(g_ref[0, rows, :])

        for h in range(heads):
            sl = slice(h * dk, (h + 1) * dk)
            Gh, qh, kh, vh = G[:, sl], qs[:, sl], kk[:, sl], v[:, sl]
            Gl = Gh[nb - 1:nb, :]
            st = st_scr[h]
            o = lax.dot_general((qh * jnp.exp(Gh)).astype(BF16), st.astype(BF16),
                                (((1,), (1,)), ((), ())), preferred_element_type=F32)
            o_top, o_bot = o[:half], o[half:]
            G_top, G_bot = Gh[:half], Gh[half:]
            q_top, q_bot = qh[:half], qh[half:]
            for s in range(nb):
                Gs, ks, vs = Gh[s:s + 1], kh[s:s + 1], vh[s:s + 1]
                if s < half:
                    a = jnp.sum(jnp.exp(G_top - Gs) * (q_top * ks), axis=-1, keepdims=True)
                    o_top = o_top + jnp.where(row8 >= s, a, 0.0) * vs
                a = jnp.sum(jnp.exp(G_bot - Gs) * (q_bot * ks), axis=-1, keepdims=True)
                if s >= half:
                    a = jnp.where(row8 >= s - half, a, 0.0)
                o_bot = o_bot + a * vs
            kd = (kh * jnp.exp(Gl - Gh)).astype(BF16)
            upd = lax.dot_general(vh.astype(BF16), kd, (((0,), (0,)), ((), ())),
                                  preferred_element_type=F32)
            st_scr[h] = st * jnp.exp(Gl) + upd

            o = jnp.concatenate([o_top, o_bot], axis=0)
            ms = jnp.mean(o * o, axis=-1, keepdims=True)
            y = (o * lax.rsqrt(ms + EPS)) * ng_ref[:, sl] * gate[:, sl]
            oh_ref[0, rows, sl] = y.astype(BF16)
        return carry

    lax.fori_loop(0, chunk // nb, block, 0)

    @pl.when(c == pl.num_programs(1) - 1)
    def _():
        for h in range(heads):
            S_ref[0, h] = st_scr[h].T


def _hgrn(z, lower_bounds, layer, norm_g, S0, col_block):
    B, T, _ = z.shape
    _, heads, dk, dv = S0.shape
    HW = heads * dk
    assert dk == dv
    Lc = HGRN_CHUNK if T % HGRN_CHUNK == 0 else T
    assert Lc % HGRN_BLOCK == 0
    depth = lower_bounds.shape[0]
    vmem = 2 * (4 * Lc * HW * 4 + Lc * HW * 2) + 5 * heads * dk * dv * 4 + 16 * MIB
    kern = functools.partial(_hgrn_kernel, heads=heads, dk=dk, dv=dv, layer=layer, chunk=Lc)
    zspec = lambda idx: pl.BlockSpec((1, Lc, HW), lambda b, c: (b, c, idx))
    oh, S = pl.pallas_call(
        kern,
        grid=(B, T // Lc),
        in_specs=[
            zspec(col_block), zspec(col_block + 1), zspec(col_block + 2), zspec(col_block + 3),
            pl.BlockSpec((depth, HW), lambda b, c: (0, 0)),
            pl.BlockSpec((1, HW), lambda b, c: (0, 0)),
            pl.BlockSpec((1, heads, dk, dv), lambda b, c: (b, 0, 0, 0)),
        ],
        out_specs=[
            pl.BlockSpec((1, Lc, HW), lambda b, c: (b, c, 0)),
            pl.BlockSpec((1, heads, dk, dv), lambda b, c: (b, 0, 0, 0)),
        ],
        out_shape=[
            jax.ShapeDtypeStruct((B, T, HW), BF16),
            jax.ShapeDtypeStruct((B, heads, dk, dv), F32),
        ],
        scratch_shapes=[pltpu.VMEM((heads, dv, dk), F32)],
        compiler_params=_params(("parallel", "arbitrary"), vmem),
        name="hgrn",
    )(z, z, z, z, lower_bounds, norm_g.reshape(1, HW), S0)
    return oh, S


def _out_kernel(hm_ref, oh_ref, wa_ref, wb_ref, x_ref, ga_ref, xo_ref):
    acc = jnp.dot(hm_ref[...], wa_ref[...], preferred_element_type=F32)
    acc = acc + jnp.dot(oh_ref[...], wb_ref[...], preferred_element_type=F32)
    xo_ref[...] = x_ref[...] + ga_ref[...] * acc.reshape(xo_ref.shape)


def _out_proj(hm, oh, w_out, x, ga, *, tm=1024, tn=1024):
    B, T, D = x.shape
    MW, HW = hm.shape[-1], oh.shape[-1]
    assert MW == HW and w_out.shape[0] == MW + HW
    G, R, TG, reps = _row_groups(B, T, tm)
    TM, M = TG * R, B * T
    tn = _col_tile(D, tn)
    vmem = 2 * (2 * TM * MW * 2 + 2 * MW * tn * 2 + 2 * TM * tn * 4 + tn * 4) + 2 * TM * tn * 4 + 4 * MIB
    return pl.pallas_call(
        _out_kernel,
        grid=(G // TG, D // tn),
        in_specs=[
            pl.BlockSpec((TM, MW), lambda i, j: (i, 0)),
            pl.BlockSpec((TM, HW), lambda i, j: (i, 0)),
            pl.BlockSpec((MW, tn), lambda i, j: (0, j)),
            pl.BlockSpec((HW, tn), lambda i, j: (1, j)),
            pl.BlockSpec((TG, R, tn), lambda i, j: (i, 0, j)),
            pl.BlockSpec((TG, 1, tn), lambda i, j: (i, 0, j)),
        ],
        out_specs=pl.BlockSpec((TG, R, tn), lambda i, j: (i, 0, j)),
        out_shape=jax.ShapeDtypeStruct((G, R, D), F32),
        compiler_params=_params(("parallel", "arbitrary"), vmem),
        name="out_proj",
    )(hm.reshape(M, MW), oh.reshape(M, HW), w_out, w_out, x.reshape(G, R, D), _per_group(ga, reps)
      ).reshape(B, T, D)


def _ffn_kernel(x_ref, sc_ref, sh_ref, g_ref, ga_ref, wu_ref, wd_ref, fg_ref, o_ref, h_scr,
                *, final, n_split):
    f = pl.program_id(1)
    D = o_ref.shape[-1]
    dn = D // n_split

    @pl.when(f == 0)
    def _():
        h = _mod_norm(x_ref[...], g_ref[...], sc_ref[...], sh_ref[...])
        h_scr[...] = h.reshape(h_scr.shape).astype(BF16)

    u = jnp.dot(h_scr[...], wu_ref[...], preferred_element_type=F32)
    a = jnp.square(jnp.maximum(u, 0.0)).astype(BF16)
    blk = o_ref.shape[:-1] + (dn,)

    @pl.when(f == 0)
    def _():
        for n in range(n_split):
            sl = slice(n * dn, (n + 1) * dn)
            o_ref[:, :, sl] = jnp.dot(a, wd_ref[:, sl], preferred_element_type=F32).reshape(blk)

    @pl.when(f > 0)
    def _():
        for n in range(n_split):
            sl = slice(n * dn, (n + 1) * dn)
            o_ref[:, :, sl] += jnp.dot(a, wd_ref[:, sl], preferred_element_type=F32).reshape(blk)

    @pl.when(f == pl.num_programs(1) - 1)
    def _():
        xo = x_ref[...] + ga_ref[...] * o_ref[...]
        if final:
            ms = jnp.mean(xo * xo, axis=-1, keepdims=True)
            xo = (xo * lax.rsqrt(ms + EPS)) * fg_ref[...]
        o_ref[...] = xo


def _ffn(x, sc, sh, g, ga, w_up, w_down, final_g, *, final, tm=512, tf=512):
    B, T, D = x.shape
    FF = w_up.shape[1]
    G, R, TG, reps = _row_groups(B, T, tm)
    TM = TG * R
    tf = _col_tile(FF, tf)
    n_split = 4 if D % (4 * LANES) == 0 else 1
    vmem = 3 * TM * D * 4 + 2 * (2 * D * tf * 2 + (3 * TG + 2) * SUBLANES * D * 4) + TM * D * 2 + TM * tf * 6 \
        + 2 * TM * (D // n_split) * 4 + 4 * MIB
    kern = functools.partial(_ffn_kernel, final=final, n_split=n_split)
    vec = lambda: pl.BlockSpec((TG, 1, D), lambda i, f: (i, 0, 0))
    one = lambda: pl.BlockSpec((1, 1, D), lambda i, f: (0, 0, 0))
    return pl.pallas_call(
        kern,
        grid=(G // TG, FF // tf),
        in_specs=[
            pl.BlockSpec((TG, R, D), lambda i, f: (i, 0, 0), pipeline_mode=pl.Buffered(1)),
            vec(), vec(), one(), vec(),
            pl.BlockSpec((D, tf), lambda i, f: (0, f)),
            pl.BlockSpec((tf, D), lambda i, f: (f, 0)),
            one(),
        ],
        out_specs=pl.BlockSpec((TG, R, D), lambda i, f: (i, 0, 0)),
        out_shape=jax.ShapeDtypeStruct((G, R, D), F32),
        scratch_shapes=[pltpu.VMEM((TM, D), BF16)],
        compiler_params=_params(("parallel", "arbitrary"), vmem),
        name="ffn",
    )(x.reshape(G, R, D), _per_group(sc, reps), _per_group(sh, reps), g.reshape(1, 1, D),
      _per_group(ga, reps), w_up, w_down, final_g.reshape(1, 1, D)).reshape(B, T, D)


def _trunk(x, mod, C0, n0, m0, S0, wts):
    depth = mod.shape[0]
    D = x.shape[-1]
    heads = C0.shape[2]
    Cs, ns, ms, Ss = [], [], [], []
    for l in range(depth):
        sh1, sc1, ga1, sh2, sc2, ga2 = (mod[l, :, i * D:(i + 1) * D] for i in range(6))
        z, zg = _in_proj(x, sc1, sh1, wts["norm1_g"][l], wts["w_main"][l], wts["w_gate"][l])
        hm, C, n, m = _mlstm(z, zg, wts["b_gate"][l], wts["mlstm_norm_g"][l], C0[l], n0[l], m0[l])
        oh, S = _hgrn(z, wts["lower_bounds"], l, wts["hgrn_norm_g"][l], S0[l], col_block=3)
        x = _out_proj(hm, oh, wts["w_out"][l], x, ga1)
        x = _ffn(x, sc2, sh2, wts["norm2_g"][l], ga2, wts["w_up"][l], wts["w_down"][l],
                 wts["final_g"], final=(l == depth - 1))
        Cs.append(C); ns.append(n); ms.append(m); Ss.append(S)
    return x, jnp.stack(Cs), jnp.stack(ns), jnp.stack(ms), jnp.stack(Ss)


def kernel(x_prompt, x_sample, state_mlstm_C, state_mlstm_n, state_mlstm_m, state_hgrn_S, c_prompt, c_sample, w_mod, b_mod, norm1_g, w_in, b_gate, lower_bounds, mlstm_norm_g, hgrn_norm_g, w_out, norm2_g, w_up, w_down, final_g):
    depth = w_mod.shape[0]
    Bp, Bs = x_prompt.shape[0], x_sample.shape[0]
    _, _, heads, dk, dv = state_mlstm_C.shape
    _, _, hh, hdk, hdv = state_hgrn_S.shape
    gate0 = 2 * heads * dk + 2 * heads * dv
    ngate = 2 * heads

    w_main = jnp.concatenate([w_in[:, :, :gate0], w_in[:, :, gate0 + ngate:]], axis=-1).astype(BF16)
    w_gate = jnp.pad(w_in[:, :, gate0:gate0 + ngate], ((0, 0), (0, 0), (0, LANES - ngate))).astype(BF16)
    wts = dict(norm1_g=norm1_g, w_main=w_main, w_gate=w_gate, b_gate=b_gate, lower_bounds=lower_bounds,
               mlstm_norm_g=mlstm_norm_g, hgrn_norm_g=hgrn_norm_g, w_out=w_out.astype(BF16),
               norm2_g=norm2_g, w_up=w_up.astype(BF16), w_down=w_down.astype(BF16), final_g=final_g)

    nb = Bp + Bs
    MP = -(-nb // BF16_ROWS) * BF16_ROWS
    c_all = jnp.pad(jnp.concatenate([c_prompt, c_sample], axis=0), ((0, MP - nb), (0, 0)))
    mod = _modulation(c_all, w_mod, b_mod)

    zC = jnp.zeros((depth, Bp, heads, dk, dv), F32)
    zn = jnp.zeros((depth, Bp, heads, dk), F32)
    zm = jnp.zeros((depth, Bp, heads), F32)
    zS = jnp.zeros((depth, Bp, hh, hdk, hdv), F32)
    y_p, pC, pn, pm, pS = _trunk(x_prompt, mod[:, :Bp], zC, zn, zm, zS, wts)
    y_s, sC, sn, sm, sS = _trunk(x_sample, mod[:, Bp:nb], state_mlstm_C, state_mlstm_n, state_mlstm_m,
                                 state_hgrn_S, wts)
    return (y_p, y_s, pC, pn, pm, pS, sC, sn, sm, sS)
```

```python
import functools

import jax
import jax.numpy as jnp
import numpy as np
from jax import lax
from jax.experimental import pallas as pl
from jax.experimental.pallas import tpu as pltpu

EPS = 1e-6
NEG_BIG = -1e30
LB_FLOOR = 1e-30
MLSTM_CHUNK = 64
MLSTM_STEP = 256
HGRN_BLOCK = 16
HGRN_CHUNK = 64
LANES = 128
SUBLANES = 8
BF16_ROWS = 16
MIB = 1024 * 1024
VMEM_CAP = 63 * MIB

F32 = jnp.float32
BF16 = jnp.bfloat16


def _params(semantics, vmem_bytes):
    return pltpu.CompilerParams(dimension_semantics=semantics,
                                vmem_limit_bytes=int(min(VMEM_CAP, vmem_bytes)))


def _sigmoid(x):
    return 1.0 / (1.0 + jnp.exp(-x))


def _silu(x):
    return x * _sigmoid(x)


def _log_sigmoid(x):
    return jnp.minimum(x, 0.0) - jnp.log1p(jnp.exp(-jnp.abs(x)))


def _lower_tri(n, dtype):
    r = lax.broadcasted_iota(jnp.int32, (n, n), 0)
    c = lax.broadcasted_iota(jnp.int32, (n, n), 1)
    return jnp.where(r >= c, 1.0, 0.0).astype(dtype)


def _cumsum_rows(x, tri):
    hi = x.astype(BF16)
    r1 = x - hi.astype(F32)
    mid = r1.astype(BF16)
    lo = (r1 - mid.astype(F32)).astype(BF16)
    dot = functools.partial(jnp.dot, preferred_element_type=F32)
    return (dot(tri, hi) + dot(tri, mid)) + dot(tri, lo)


def _col_tile(n, pref):
    return max(t for t in range(LANES, min(pref, n) + 1, LANES) if n % t == 0)


def _row_groups(B, T, tm_pref):
    if T >= tm_pref:
        assert T % tm_pref == 0
        R, TG = tm_pref, 1
    else:
        R = T
        TG = max(1, min(B, tm_pref // T))
        assert B % TG == 0
    return (B * T) // R, R, TG, T // R


def _for_row_chunks(TG, R, body, step=4 * BF16_ROWS):
    step = step if R % step == 0 else R
    per = R // step

    def piece(c, carry):
        t = c // per
        r = pl.multiple_of((c % per) * step, step)
        body(pl.ds(t, 1), pl.ds(r, step), pl.ds(pl.multiple_of(t * R + r, step), step))
        return carry

    lax.fori_loop(0, TG * per, piece, 0)


def _mod_spec(mod, layer, row0, part, TG, reps, width, once=True):
    D = mod.shape[-1] // 6
    per = D // width
    mode = dict(pipeline_mode=pl.Buffered(1)) if once else {}
    col = (lambda j: part) if per == 1 else (lambda j: part * per + j)
    if reps > 1:
        assert TG == 1
        return pl.BlockSpec((None, 1, 1, width), lambda i, j: (layer, row0 + i // reps, 0, col(j)), **mode)
    assert row0 % TG == 0
    return pl.BlockSpec((None, TG, 1, width), lambda i, j: (layer, row0 // TG + i, 0, col(j)), **mode)


def _mod_norm(x, g, sc, sh):
    ms = jnp.mean(x * x, axis=-1, keepdims=True)
    return (x * lax.rsqrt(ms + EPS) * g) * (1.0 + sc) + sh


def _mod_kernel(c_ref, w_ref, b_ref, o_ref):
    c = c_ref[...]
    a = _silu(c).astype(BF16)
    w = w_ref[0].astype(BF16)
    o_ref[0] = jnp.dot(a, w, preferred_element_type=F32) + b_ref[0]


def _modulation(c_all, w_mod, b_mod, *, tn=512):
    depth, D, N = w_mod.shape
    MP = c_all.shape[0]
    tn = _col_tile(N, tn)
    vmem = 2 * (D * tn * 4 + MP * tn * 4 + tn * 4) + 2 * MP * D * 4 + D * tn * 2 + 4 * MIB
    return pl.pallas_call(
        _mod_kernel,
        grid=(depth, N // tn),
        in_specs=[
            pl.BlockSpec((MP, D), lambda l, j: (0, 0)),
            pl.BlockSpec((1, D, tn), lambda l, j: (l, 0, j)),
            pl.BlockSpec((1, 1, tn), lambda l, j: (l, 0, j)),
        ],
        out_specs=pl.BlockSpec((1, MP, tn), lambda l, j: (l, 0, j)),
        out_shape=jax.ShapeDtypeStruct((depth, MP, N), F32),
        compiler_params=_params(("parallel", "parallel"), vmem),
        name="modulation",
    )(c_all, w_mod, b_mod.reshape(depth, 1, N))


def _in_kernel(x_ref, sc_ref, sh_ref, g_ref, w_ref, wg_ref, z_ref, zg_ref, h_scr):
    @pl.when(pl.program_id(1) == 0)
    def _():
        def piece(gs, rs, fs):
            h = _mod_norm(x_ref[gs, rs, :], g_ref[...], sc_ref[gs], sh_ref[gs])
            hb = h.reshape(-1, h.shape[-1]).astype(BF16)
            h_scr[fs, :] = hb
            zg_ref[fs, :] = jnp.dot(hb, wg_ref[...], preferred_element_type=F32)

        _for_row_chunks(*x_ref.shape[:2], piece)

    z_ref[...] = jnp.dot(h_scr[...], w_ref[...], preferred_element_type=F32)


def _in_proj(x, mod, row0, g, w_main, w_gate, layer, *, tm=512, tn=2048):
    B, T, D = x.shape
    NZ = w_main.shape[-1]
    G, R, TG, reps = _row_groups(B, T, tm)
    TM, M = TG * R, B * T
    tn = _col_tile(NZ, tn)
    vmem = TM * D * 4 + D * LANES * 2 + (2 * TG + 1) * SUBLANES * D * 4 \
        + 2 * (D * tn * 2 + TM * tn * 4 + TM * LANES * 4) + TM * D * 2 + TM * D * 4 + 4 * MIB
    once = pl.Buffered(1)
    z, zg = pl.pallas_call(
        _in_kernel,
        grid=(G // TG, NZ // tn),
        in_specs=[
            pl.BlockSpec((TG, R, D), lambda i, j: (i, 0, 0), pipeline_mode=once),
            _mod_spec(mod, layer, row0, 1, TG, reps, D),
            _mod_spec(mod, layer, row0, 0, TG, reps, D),
            pl.BlockSpec((None, 1, D), lambda i, j: (layer, 0, 0), pipeline_mode=once),
            pl.BlockSpec((None, D, tn), lambda i, j: (layer, 0, j)),
            pl.BlockSpec((None, D, LANES), lambda i, j: (layer, 0, 0), pipeline_mode=once),
        ],
        out_specs=[
            pl.BlockSpec((TM, tn), lambda i, j: (i, j)),
            pl.BlockSpec((TM, LANES), lambda i, j: (i, 0)),
        ],
        out_shape=[jax.ShapeDtypeStruct((M, NZ), F32), jax.ShapeDtypeStruct((M, LANES), F32)],
        scratch_shapes=[pltpu.VMEM((TM, D), BF16)],
        compiler_params=_params(("parallel", "arbitrary"), vmem),
        name="in_proj",
    )(x.reshape(G, R, D), mod, mod, g.reshape(-1, 1, D), w_main, w_gate)
    return z.reshape(B, T, NZ), zg.reshape(B, T, LANES)


def _mlstm_kernel(q_ref, k_ref, v_ref, o_ref, zg_ref, bg_ref, mg_ref, C0_ref, n0_ref, m0_ref,
                  hm_ref, C_ref, n_ref, m_ref, *, heads, dk, dv, chunk, nsub):
    L = chunk

    @pl.when(pl.program_id(1) == 0)
    def _():
        C_ref[...] = C0_ref[0]
        n_ref[...] = n0_ref[0]
        m_ref[...] = m0_ref[0]

    tri = _lower_tri(L, BF16)
    causal = (lax.broadcasted_iota(jnp.int32, (L, L), 0) >= lax.broadcasted_iota(jnp.int32, (L, L), 1))
    for s in range(nsub):
        _mlstm_chunk(slice(s * L, (s + 1) * L), tri, causal, q_ref, k_ref, v_ref, o_ref, zg_ref, bg_ref, mg_ref,
                     hm_ref, C_ref, n_ref, m_ref, heads=heads, dk=dk, dv=dv, L=L)


def _mlstm_chunk(r, tri, causal, q_ref, k_ref, v_ref, o_ref, zg_ref, bg_ref, mg_ref,
                 hm_ref, C_ref, n_ref, m_ref, *, heads, dk, dv, L):
    gates = zg_ref[0, r, :] + bg_ref[...]
    bsum = _cumsum_rows(_log_sigmoid(gates), tri)
    lane = lax.broadcasted_iota(jnp.int32, gates.shape, 1)
    rows = jnp.where(lane < heads, gates, bsum).T

    for h in range(heads):
        i_col = gates[:, h:h + 1]
        b_col = bsum[:, heads + h:heads + h + 1]
        i_row = rows[h:h + 1, :]
        b_row = rows[heads + h:heads + h + 1, :]
        m_prev = m_ref[0, h:h + 1, 0:1]
        n_prev = n_ref[0, h:h + 1, :]
        C_prev = C_ref[0, h]

        q = q_ref[0, r, h * dk:(h + 1) * dk] * (dk ** -0.5)
        k = k_ref[0, r, h * dk:(h + 1) * dk]
        vb = v_ref[0, r, h * dv:(h + 1) * dv].astype(BF16)
        qb = q.astype(BF16)

        D = jnp.where(causal, (b_col - b_row) + i_row, NEG_BIG)
        inter = b_col + m_prev
        mt = jnp.maximum(inter, jnp.max(D, axis=-1, keepdims=True))
        a_inter = jnp.exp(inter - mt)
        S = lax.dot_general(qb, k.astype(BF16), (((1,), (1,)), ((), ())),
                            preferred_element_type=F32) * jnp.exp(D - mt)
        num = a_inter * jnp.dot(qb, C_prev.astype(BF16), preferred_element_type=F32) \
            + jnp.dot(S.astype(BF16), vb, preferred_element_type=F32)
        den = a_inter * jnp.sum(q * n_prev, axis=-1, keepdims=True) + jnp.sum(S, axis=-1, keepdims=True)
        hh = num / jnp.maximum(jnp.abs(den), jnp.exp(-mt))

        b_last = b_col[L - 1:L, :]
        wlast = (b_last - b_col) + i_col
        m_new = jnp.maximum(b_last + m_prev, jnp.max(wlast, axis=0, keepdims=True))
        a_c = jnp.exp((b_last + m_prev) - m_new)
        kw = k * jnp.exp(wlast - m_new)
        C_ref[0, h] = a_c * C_prev + lax.dot_general(kw.astype(BF16), vb, (((0,), (0,)), ((), ())),
                                                     preferred_element_type=F32)
        n_ref[0, h:h + 1, :] = a_c * n_prev + jnp.sum(kw, axis=0, keepdims=True)
        m_ref[0, h:h + 1, :] = jnp.broadcast_to(m_new, (1, LANES))

        ms = jnp.mean(hh * hh, axis=-1, keepdims=True)
        y = (hh * lax.rsqrt(ms + EPS)) * mg_ref[:, h * dv:(h + 1) * dv]
        y = y * _sigmoid(o_ref[0, r, h * dv:(h + 1) * dv])
        hm_ref[0, r, h * dv:(h + 1) * dv] = y.astype(BF16)


def _mlstm(z, zg, b_gate, norm_g, C0, n0, m0, sl):
    B, T, _ = z.shape
    _, _, heads, dk, dv = C0.shape
    QW, VW = heads * dk, heads * dv
    assert VW == 2 * QW and 2 * heads <= LANES
    L = MLSTM_CHUNK if T % MLSTM_CHUNK == 0 else T
    Ls = MLSTM_STEP if (T % MLSTM_STEP == 0 and MLSTM_STEP % L == 0) else L
    bias = jnp.zeros((1, LANES), F32).at[0, :2 * heads].set(b_gate)
    m0b = jnp.broadcast_to(m0[..., None], m0.shape + (LANES,))
    vmem = 2 * (Ls * (2 * QW + 2 * VW + LANES) * 4 + Ls * VW * 2) + 4 * heads * dk * dv * 4 + 12 * MIB
    kern = functools.partial(_mlstm_kernel, heads=heads, dk=dk, dv=dv, chunk=L, nsub=Ls // L)
    hm, C, n, m = pl.pallas_call(
        kern,
        grid=(B, T // Ls),
        in_specs=[
            pl.BlockSpec((1, Ls, QW), lambda b, c: (b, c, 0)),
            pl.BlockSpec((1, Ls, QW), lambda b, c: (b, c, 1)),
            pl.BlockSpec((1, Ls, VW), lambda b, c: (b, c, 1)),
            pl.BlockSpec((1, Ls, VW), lambda b, c: (b, c, 2)),
            pl.BlockSpec((1, Ls, LANES), lambda b, c: (b, c, 0)),
            pl.BlockSpec((1, LANES), lambda b, c: (0, 0)),
            pl.BlockSpec((1, VW), lambda b, c: (0, 0)),
            pl.BlockSpec((1, 1, heads, dk, dv), lambda b, c: (sl, b, 0, 0, 0)),
            pl.BlockSpec((1, 1, heads, dk), lambda b, c: (sl, b, 0, 0)),
            pl.BlockSpec((1, 1, heads, LANES), lambda b, c: (sl, b, 0, 0)),
        ],
        out_specs=[
            pl.BlockSpec((1, Ls, VW), lambda b, c: (b, c, 0)),
            pl.BlockSpec((1, heads, dk, dv), lambda b, c: (b, 0, 0, 0)),
            pl.BlockSpec((1, heads, dk), lambda b, c: (b, 0, 0)),
            pl.BlockSpec((1, heads, LANES), lambda b, c: (b, 0, 0)),
        ],
        out_shape=[
            jax.ShapeDtypeStruct((B, T, VW), BF16),
            jax.ShapeDtypeStruct((B, heads, dk, dv), F32),
            jax.ShapeDtypeStruct((B, heads, dk), F32),
            jax.ShapeDtypeStruct((B, heads, LANES), F32),
        ],
        compiler_params=_params(("parallel", "arbitrary"), vmem),
        name="mlstm",
    )(z, z, z, z, zg, bias, norm_g.reshape(1, VW), C0, n0, m0b)
    return hm, C, n, m[:, :, 0]


def _hgrn_tables(nb):
    u = np.arange(nb)[:, None]
    j = np.arange(nb)[None, :]
    mats, masks = [j <= u], []
    w = nb // 2
    while w >= 1:
        r = (u // (2 * w)) * (2 * w) + w - 1
        upper = (u % (2 * w)) >= w
        mats.append(np.where(upper, (j > r) & (j <= u), (j > u) & (j <= r)))
        masks.append(((u // (2 * w)) == (j // (2 * w))) & upper & ((j % (2 * w)) < w))
        w //= 2
    mats.append(j > u)
    masks.append(u == j)
    return np.concatenate(mats, 0).astype(np.float32), np.stack(masks).astype(np.float32)


def _hgrn_kernel(q_ref, f_ref, i_ref, g_ref, lb_ref, ng_ref, S0_ref, mst_ref, lvl_ref, oh_ref, S_ref, st_scr,
                 *, heads, dk, dv, layer, chunk):
    nb = HGRN_BLOCK
    nlev = lvl_ref.shape[0]
    c = pl.program_id(1)

    @pl.when(c == 0)
    def _():
        for h in range(heads):
            st_scr[h] = S0_ref[0, 0, h].T

    lbs = lb_ref[...]
    e = jnp.exp(lbs - jnp.max(lbs, axis=0, keepdims=True))
    sm = e / jnp.sum(e, axis=0, keepdims=True)
    cum = sm[0:1, :]
    for j in range(1, layer + 1):
        cum = cum + sm[j:j + 1, :]
    lb = cum - sm[0:1, :]
    lb_floor = jnp.maximum(lb, LB_FLOOR)
    one_m = 1.0 - lb

    mst = mst_ref[...]
    masks = [lvl_ref[i] > 0.5 for i in range(nlev)]
    nt = (((1,), (1,)), ((), ()))

    def block(j, carry):
        rows = pl.ds(j * nb, nb)
        fx = f_ref[0, rows, :]
        en = jnp.exp(-jnp.abs(fx))
        rc = 1.0 / (1.0 + en)
        er = en * rc
        pos = fx >= 0.0
        g = jnp.log(lb_floor + one_m * jnp.where(pos, rc, er))
        kk = one_m * jnp.where(pos, er, rc)
        qs = _silu(q_ref[0, rows, :])
        vb = i_ref[0, rows, :].astype(BF16)
        gate = _silu(g_ref[0, rows, :])

        ghi = g.astype(BF16)
        gmid = (g - ghi.astype(F32)).astype(BF16)
        E = jnp.exp(jnp.dot(mst, ghi, preferred_element_type=F32)
                    + jnp.dot(mst, gmid, preferred_element_type=F32))
        EG = E[:nb]
        qG = (qs * EG).astype(BF16)
        kL = (kk * E[nlev * nb:]).astype(BF16)
        Ql = [(qs * E[(i + 1) * nb:(i + 2) * nb]).astype(BF16) for i in range(nlev - 1)] + [qs.astype(BF16)]
        Kl = [(kk * E[(i + 1) * nb:(i + 2) * nb]).astype(BF16) for i in range(nlev - 1)] + [kk.astype(BF16)]
        decay = EG[nb - 1:nb]

        for h in range(heads):
            sl = slice(h * dk, (h + 1) * dk)
            st = st_scr[h]
            o = lax.dot_general(qG[:, sl], st.astype(BF16), nt, preferred_element_type=F32)
            A = jnp.zeros((nb, nb), F32)
            for i in range(nlev):
                A = jnp.where(masks[i], lax.dot_general(Ql[i][:, sl], Kl[i][:, sl], nt,
                                                        preferred_element_type=F32), A)
            o = o + jnp.dot(A.astype(BF16), vb[:, sl], preferred_element_type=F32)
            upd = lax.dot_general(vb[:, sl], kL[:, sl], (((0,), (0,)), ((), ())),
                                  preferred_element_type=F32)
            st_scr[h] = st * decay[:, sl] + upd

            ms = jnp.mean(o * o, axis=-1, keepdims=True)
            y = (o * lax.rsqrt(ms + EPS)) * ng_ref[:, sl] * gate[:, sl]
            oh_ref[0, rows, sl] = y.astype(BF16)
        return carry

    for j in range(chunk // nb):
        block(j, 0)

    @pl.when(c == pl.num_programs(1) - 1)
    def _():
        for h in range(heads):
            S_ref[0, h] = st_scr[h].T


def _hgrn(z, lower_bounds, layer, norm_g, S0, sl, col_block):
    B, T, _ = z.shape
    _, _, heads, dk, dv = S0.shape
    HW = heads * dk
    assert dk == dv
    Lc = HGRN_CHUNK if T % HGRN_CHUNK == 0 else T
    assert Lc % HGRN_BLOCK == 0
    depth = lower_bounds.shape[0]
    mst, lvl = _hgrn_tables(HGRN_BLOCK)
    mst = jnp.asarray(mst, BF16)
    lvl = jnp.asarray(lvl, F32)
    vmem = 2 * (4 * Lc * HW * 4 + Lc * HW * 2) + 5 * heads * dk * dv * 4 \
        + 6 * mst.shape[0] * HW * 4 + 16 * MIB
    kern = functools.partial(_hgrn_kernel, heads=heads, dk=dk, dv=dv, layer=layer, chunk=Lc)
    zspec = lambda idx: pl.BlockSpec((1, Lc, HW), lambda b, c: (b, c, idx))
    oh, S = pl.pallas_call(
        kern,
        grid=(B, T // Lc),
        in_specs=[
            zspec(col_block), zspec(col_block + 1), zspec(col_block + 2), zspec(col_block + 3),
            pl.BlockSpec((depth, HW), lambda b, c: (0, 0)),
            pl.BlockSpec((1, HW), lambda b, c: (0, 0)),
            pl.BlockSpec((1, 1, heads, dk, dv), lambda b, c: (sl, b, 0, 0, 0)),
            pl.BlockSpec(mst.shape, lambda b, c: (0, 0)),
            pl.BlockSpec(lvl.shape, lambda b, c: (0, 0, 0)),
        ],
        out_specs=[
            pl.BlockSpec((1, Lc, HW), lambda b, c: (b, c, 0)),
            pl.BlockSpec((1, heads, dk, dv), lambda b, c: (b, 0, 0, 0)),
        ],
        out_shape=[
            jax.ShapeDtypeStruct((B, T, HW), BF16),
            jax.ShapeDtypeStruct((B, heads, dk, dv), F32),
        ],
        scratch_shapes=[pltpu.VMEM((heads, dv, dk), F32)],
        compiler_params=_params(("parallel", "arbitrary"), vmem),
        name="hgrn",
    )(z, z, z, z, lower_bounds, norm_g.reshape(1, HW), S0, mst, lvl)
    return oh, S


def _out_kernel(hm_ref, oh_ref, wa_ref, wb_ref, x_ref, ga_ref, xo_ref):
    acc = jnp.dot(hm_ref[...], wa_ref[...], preferred_element_type=F32)
    acc = acc + jnp.dot(oh_ref[...], wb_ref[...], preferred_element_type=F32)
    xo_ref[...] = x_ref[...] + ga_ref[...] * acc.reshape(xo_ref.shape)


def _out_proj(hm, oh, w_out, layer, x, mod, row0, *, tm=1024, tn=1024):
    B, T, D = x.shape
    MW, HW = hm.shape[-1], oh.shape[-1]
    assert MW == HW and w_out.shape[1] == MW + HW
    G, R, TG, reps = _row_groups(B, T, tm)
    TM, M = TG * R, B * T
    tn = _col_tile(D, tn)
    vmem = 2 * (2 * TM * MW * 2 + 2 * MW * tn * 2 + 2 * TM * tn * 4 + tn * 4) + 2 * TM * tn * 4 + 4 * MIB
    return pl.pallas_call(
        _out_kernel,
        grid=(G // TG, D // tn),
        in_specs=[
            pl.BlockSpec((TM, MW), lambda i, j: (i, 0)),
            pl.BlockSpec((TM, HW), lambda i, j: (i, 0)),
            pl.BlockSpec((None, MW, tn), lambda i, j: (layer, 0, j)),
            pl.BlockSpec((None, HW, tn), lambda i, j: (layer, 1, j)),
            pl.BlockSpec((TG, R, tn), lambda i, j: (i, 0, j)),
            _mod_spec(mod, layer, row0, 2, TG, reps, tn, once=False),
        ],
        out_specs=pl.BlockSpec((TG, R, tn), lambda i, j: (i, 0, j)),
        out_shape=jax.ShapeDtypeStruct((G, R, D), F32),
        compiler_params=_params(("parallel", "arbitrary"), vmem),
        name="out_proj",
    )(hm.reshape(M, MW), oh.reshape(M, HW), w_out, w_out, x.reshape(G, R, D), mod).reshape(B, T, D)


def _ffn_kernel(x_ref, sc_ref, sh_ref, g_ref, ga_ref, wu_ref, wd_ref, fg_ref, o_ref, h_scr,
                *, final, n_split, f_split):
    f = pl.program_id(1)
    D = o_ref.shape[-1]
    dn = D // n_split
    ts = wu_ref.shape[-1] // f_split

    @pl.when(f == 0)
    def _():
        def piece(gs, rs, fs):
            h = _mod_norm(x_ref[gs, rs, :], g_ref[...], sc_ref[gs], sh_ref[gs])
            h_scr[fs, :] = h.reshape(-1, D).astype(BF16)
            o_ref[gs, rs, :] = jnp.zeros_like(h)

        _for_row_chunks(*x_ref.shape[:2], piece)

    blk = o_ref.shape[:-1] + (dn,)
    for k in range(f_split):
        u = jnp.dot(h_scr[...], wu_ref[:, k * ts:(k + 1) * ts], preferred_element_type=F32)
        a = jnp.square(jnp.maximum(u, 0.0)).astype(BF16)
        for n in range(n_split):
            sl = slice(n * dn, (n + 1) * dn)
            o_ref[:, :, sl] += jnp.dot(a, wd_ref[k * ts:(k + 1) * ts, sl], preferred_element_type=F32).reshape(blk)

    @pl.when(f == pl.num_programs(1) - 1)
    def _():
        def piece(gs, rs, fs):
            xo = x_ref[gs, rs, :] + ga_ref[gs] * o_ref[gs, rs, :]
            if final:
                ms = jnp.mean(xo * xo, axis=-1, keepdims=True)
                xo = (xo * lax.rsqrt(ms + EPS)) * fg_ref[...]
            o_ref[gs, rs, :] = xo

        _for_row_chunks(*x_ref.shape[:2], piece)


def _ffn(x, mod, row0, g, w_up, w_down, layer, final_g, *, final, tm=512, tf=1024):
    B, T, D = x.shape
    FF = w_up.shape[-1]
    G, R, TG, reps = _row_groups(B, T, tm)
    TM = TG * R
    tf = _col_tile(FF, tf)
    n_split = 4 if D % (4 * LANES) == 0 else 1
    f_split = 2 if tf % (2 * LANES) == 0 else 1
    vmem = 2 * TM * D * 4 + 2 * 2 * D * tf * 2 + (3 * TG + 4) * SUBLANES * D * 4 + TM * D * 2 + TM * tf * 6 \
        + 2 * TM * (D // n_split) * 4 + 4 * MIB
    kern = functools.partial(_ffn_kernel, final=final, n_split=n_split, f_split=f_split)
    once = pl.Buffered(1)
    vec = lambda part: _mod_spec(mod, layer, row0, part, TG, reps, D)
    return pl.pallas_call(
        kern,
        grid=(G // TG, FF // tf),
        in_specs=[
            pl.BlockSpec((TG, R, D), lambda i, f: (i, 0, 0), pipeline_mode=once),
            vec(4), vec(3),
            pl.BlockSpec((None, 1, D), lambda i, f: (layer, 0, 0), pipeline_mode=once),
            vec(5),
            pl.BlockSpec((None, D, tf), lambda i, f: (layer, 0, f)),
            pl.BlockSpec((None, tf, D), lambda i, f: (layer, f, 0)),
            pl.BlockSpec((1, D), lambda i, f: (0, 0), pipeline_mode=once),
        ],
        out_specs=pl.BlockSpec((TG, R, D), lambda i, f: (i, 0, 0), pipeline_mode=once),
        out_shape=jax.ShapeDtypeStruct((G, R, D), F32),
        scratch_shapes=[pltpu.VMEM((TM, D), BF16)],
        compiler_params=_params(("parallel", "arbitrary"), vmem),
        name="ffn",
    )(x.reshape(G, R, D), mod, mod, g.reshape(-1, 1, D), mod, w_up, w_down,
      final_g.reshape(1, D)).reshape(B, T, D)


def _trunk(x, mod, row0, C0, n0, m0, S0, per_layer_state, wts):
    depth = mod.shape[0]
    Cs, ns, ms, Ss = [], [], [], []
    for l in range(depth):
        sl = l if per_layer_state else 0
        z, zg = _in_proj(x, mod, row0, wts["norm1_g"], wts["w_main"], wts["w_gate"], l)
        hm, C, n, m = _mlstm(z, zg, wts["b_gate"][l], wts["mlstm_norm_g"][l], C0, n0, m0, sl)
        oh, S = _hgrn(z, wts["lower_bounds"], l, wts["hgrn_norm_g"][l], S0, sl, col_block=3)
        x = _out_proj(hm, oh, wts["w_out"], l, x, mod, row0)
        x = _ffn(x, mod, row0, wts["norm2_g"], wts["w_up"], wts["w_down"], l,
                 wts["final_g"], final=(l == depth - 1))
        Cs.append(C); ns.append(n); ms.append(m); Ss.append(S)
    return x, jnp.stack(Cs), jnp.stack(ns), jnp.stack(ms), jnp.stack(Ss)


def kernel(x_prompt, x_sample, state_mlstm_C, state_mlstm_n, state_mlstm_m, state_hgrn_S, c_prompt, c_sample, w_mod, b_mod, norm1_g, w_in, b_gate, lower_bounds, mlstm_norm_g, hgrn_norm_g, w_out, norm2_g, w_up, w_down, final_g):
    depth = w_mod.shape[0]
    Bp, Bs = x_prompt.shape[0], x_sample.shape[0]
    _, _, heads, dk, dv = state_mlstm_C.shape
    _, _, hh, hdk, hdv = state_hgrn_S.shape
    gate0 = 2 * heads * dk + 2 * heads * dv
    ngate = 2 * heads

    w_main = jnp.concatenate([w_in[:, :, :gate0], w_in[:, :, gate0 + ngate:]], axis=-1).astype(BF16)
    w_gate = jnp.pad(w_in[:, :, gate0:gate0 + ngate], ((0, 0), (0, 0), (0, LANES - ngate))).astype(BF16)
    wts = dict(norm1_g=norm1_g, w_main=w_main, w_gate=w_gate, b_gate=b_gate, lower_bounds=lower_bounds,
               mlstm_norm_g=mlstm_norm_g, hgrn_norm_g=hgrn_norm_g, w_out=w_out.astype(BF16),
               norm2_g=norm2_g, w_up=w_up.astype(BF16), w_down=w_down.astype(BF16), final_g=final_g)

    pad_rows = lambda c: jnp.pad(c, ((0, -c.shape[0] % BF16_ROWS), (0, 0)))
    c_all = jnp.concatenate([pad_rows(c_prompt), pad_rows(c_sample)], axis=0)
    row_s = Bp + (-Bp % BF16_ROWS)
    mod = _modulation(c_all, w_mod, b_mod)
    mod = mod.reshape(depth, c_all.shape[0], 1, mod.shape[-1])

    zC = jnp.zeros((1, Bp, heads, dk, dv), F32)
    zn = jnp.zeros((1, Bp, heads, dk), F32)
    zm = jnp.zeros((1, Bp, heads), F32)
    zS = jnp.zeros((1, Bp, hh, hdk, hdv), F32)
    y_p, pC, pn, pm, pS = _trunk(x_prompt, mod, 0, zC, zn, zm, zS, False, wts)
    y_s, sC, sn, sm, sS = _trunk(x_sample, mod, row_s, state_mlstm_C, state_mlstm_n, state_mlstm_m,
                                 state_hgrn_S, True, wts)
    return (y_p, y_s, pC, pn, pm, pS, sC, sn, sm, sS)
```

```python
import functools

import jax
import jax.numpy as jnp
import numpy as np
from jax import lax
from jax.experimental import pallas as pl
from jax.experimental.pallas import tpu as pltpu

EPS = 1e-6
NEG_BIG = -1e30
LB_FLOOR = 1e-30
MLSTM_CHUNK = 256
MLSTM_STEP = 512
MLSTM_BATCH = 1
HGRN_BLOCK = 16
HGRN_CHUNK = 64
LANES = 128
SUBLANES = 8
BF16_ROWS = 16
MIB = 1024 * 1024
VMEM_CAP = 63 * MIB

F32 = jnp.float32
BF16 = jnp.bfloat16


def _params(semantics, vmem_bytes):
    return pltpu.CompilerParams(dimension_semantics=semantics,
                                vmem_limit_bytes=int(min(VMEM_CAP, vmem_bytes)))


def _sigmoid(x):
    return 1.0 / (1.0 + jnp.exp(-x))


def _silu(x):
    return x * _sigmoid(x)


def _log_sigmoid(x):
    return jnp.minimum(x, 0.0) - jnp.log1p(jnp.exp(-jnp.abs(x)))


def _lower_tri(n, dtype):
    r = lax.broadcasted_iota(jnp.int32, (n, n), 0)
    c = lax.broadcasted_iota(jnp.int32, (n, n), 1)
    return jnp.where(r >= c, 1.0, 0.0).astype(dtype)


def _cumsum_rows(x, tri):
    hi = x.astype(BF16)
    r1 = x - hi.astype(F32)
    mid = r1.astype(BF16)
    lo = (r1 - mid.astype(F32)).astype(BF16)
    dot = functools.partial(jnp.dot, preferred_element_type=F32)
    return (dot(tri, hi) + dot(tri, mid)) + dot(tri, lo)


def _col_tile(n, pref):
    return max(t for t in range(LANES, min(pref, n) + 1, LANES) if n % t == 0)


def _row_groups(B, T, tm_pref):
    if T >= tm_pref:
        assert T % tm_pref == 0
        R, TG = tm_pref, 1
    else:
        R = T
        TG = max(1, min(B, tm_pref // T))
        assert B % TG == 0
    return (B * T) // R, R, TG, T // R


def _for_row_chunks(TG, R, body, step=4 * BF16_ROWS):
    step = step if R % step == 0 else R
    per = R // step

    def piece(c, carry):
        t = c // per
        r = pl.multiple_of((c % per) * step, step)
        body(pl.ds(t, 1), pl.ds(r, step), pl.ds(pl.multiple_of(t * R + r, step), step))
        return carry

    lax.fori_loop(0, TG * per, piece, 0)


def _mod_spec(mod, layer, row0, part, TG, reps, width, once=True):
    D = mod.shape[-1] // 6
    per = D // width
    mode = dict(pipeline_mode=pl.Buffered(1)) if once else {}
    col = (lambda j: part) if per == 1 else (lambda j: part * per + j)
    if reps > 1:
        assert TG == 1
        return pl.BlockSpec((None, 1, 1, width), lambda i, j: (layer, row0 + i // reps, 0, col(j)), **mode)
    assert row0 % TG == 0
    return pl.BlockSpec((None, TG, 1, width), lambda i, j: (layer, row0 // TG + i, 0, col(j)), **mode)


def _weight_spec(w, layer, block, index):
    if w.ndim == 2:
        return pl.BlockSpec(block, index)
    return pl.BlockSpec((None,) + block, lambda i, j: (layer,) + index(i, j))


def _mod_norm(x, g, sc, sh):
    ms = jnp.mean(x * x, axis=-1, keepdims=True)
    return (x * lax.rsqrt(ms + EPS) * g) * (1.0 + sc) + sh


def _mod_kernel(c_ref, w_ref, b_ref, o_ref):
    c = c_ref[...]
    a = _silu(c).astype(BF16)
    w = w_ref[0].astype(BF16)
    o_ref[0] = jnp.dot(a, w, preferred_element_type=F32) + b_ref[0]


def _modulation(c_all, w_mod, b_mod, *, tn=512):
    depth, D, N = w_mod.shape
    MP = c_all.shape[0]
    tn = _col_tile(N, tn)
    vmem = 2 * (D * tn * 4 + MP * tn * 4 + tn * 4) + 2 * MP * D * 4 + D * tn * 2 + 4 * MIB
    return pl.pallas_call(
        _mod_kernel,
        grid=(depth, N // tn),
        in_specs=[
            pl.BlockSpec((MP, D), lambda l, j: (0, 0)),
            pl.BlockSpec((1, D, tn), lambda l, j: (l, 0, j)),
            pl.BlockSpec((1, 1, tn), lambda l, j: (l, 0, j)),
        ],
        out_specs=pl.BlockSpec((1, MP, tn), lambda l, j: (l, 0, j)),
        out_shape=jax.ShapeDtypeStruct((depth, MP, N), F32),
        compiler_params=_params(("parallel", "parallel"), vmem),
        name="modulation",
    )(c_all, w_mod, b_mod.reshape(depth, 1, N))


def _in_kernel(x_ref, sc_ref, sh_ref, g_ref, w_ref, wg_ref, z_ref, zg_ref, h_scr):
    @pl.when(pl.program_id(1) == 0)
    def _():
        def piece(gs, rs, fs):
            h = _mod_norm(x_ref[gs, rs, :], g_ref[...], sc_ref[gs], sh_ref[gs])
            hb = h.reshape(-1, h.shape[-1]).astype(BF16)
            h_scr[fs, :] = hb
            zg_ref[fs, :] = jnp.dot(hb, wg_ref[...], preferred_element_type=F32)

        _for_row_chunks(*x_ref.shape[:2], piece)

    z_ref[...] = jnp.dot(h_scr[...], w_ref[...], preferred_element_type=F32)


def _in_proj(x, mod, row0, g, w_main, w_gate, layer, *, tm=512, tn=1024):
    B, T, D = x.shape
    NZ = w_main.shape[-1]
    G, R, TG, reps = _row_groups(B, T, tm)
    TM, M = TG * R, B * T
    tn = _col_tile(NZ, tn)
    vmem = 2 * TM * D * 4 + D * LANES * 2 + (2 * TG + 1) * SUBLANES * D * 4 \
        + 2 * (D * tn * 2 + TM * tn * 4 + TM * LANES * 4) + TM * D * 2 + 8 * MIB
    once = pl.Buffered(1)
    z, zg = pl.pallas_call(
        _in_kernel,
        grid=(G // TG, NZ // tn),
        in_specs=[
            pl.BlockSpec((TG, R, D), lambda i, j: (i, 0, 0)),
            _mod_spec(mod, layer, row0, 1, TG, reps, D),
            _mod_spec(mod, layer, row0, 0, TG, reps, D),
            pl.BlockSpec((None, 1, D), lambda i, j: (layer, 0, 0), pipeline_mode=once),
            pl.BlockSpec((None, D, tn), lambda i, j: (layer, 0, j)),
            pl.BlockSpec((None, D, LANES), lambda i, j: (layer, 0, 0), pipeline_mode=once),
        ],
        out_specs=[
            pl.BlockSpec((TM, tn), lambda i, j: (i, j)),
            pl.BlockSpec((TM, LANES), lambda i, j: (i, 0)),
        ],
        out_shape=[jax.ShapeDtypeStruct((M, NZ), F32), jax.ShapeDtypeStruct((M, LANES), F32)],
        scratch_shapes=[pltpu.VMEM((TM, D), BF16)],
        compiler_params=_params(("parallel", "arbitrary"), vmem),
        name="in_proj",
    )(x.reshape(G, R, D), mod, mod, g.reshape(-1, 1, D), w_main, w_gate)
    return z.reshape(B, T, NZ), zg.reshape(B, T, LANES)


def _mlstm_kernel(q_ref, k_ref, v_ref, o_ref, zg_ref, bg_ref, mg_ref, C0_ref, n0_ref, m0_ref,
                  hm_ref, C_ref, n_ref, m_ref, *, heads, dk, dv, chunk, nsub):
    L = chunk

    @pl.when(pl.program_id(1) == 0)
    def _():
        C_ref[...] = C0_ref[0]
        n_ref[...] = n0_ref[0]
        m_ref[...] = m0_ref[0]

    tri = _lower_tri(L, BF16)
    causal = (lax.broadcasted_iota(jnp.int32, (L, L), 0) >= lax.broadcasted_iota(jnp.int32, (L, L), 1))
    for s in range(nsub):
        for bb in range(q_ref.shape[0]):
            _mlstm_chunk(bb, slice(s * L, (s + 1) * L), tri, causal, q_ref, k_ref, v_ref, o_ref, zg_ref, bg_ref,
                         mg_ref, hm_ref, C_ref, n_ref, m_ref, heads=heads, dk=dk, dv=dv, L=L)


def _mlstm_chunk(bb, r, tri, causal, q_ref, k_ref, v_ref, o_ref, zg_ref, bg_ref, mg_ref,
                 hm_ref, C_ref, n_ref, m_ref, *, heads, dk, dv, L):
    gates = zg_ref[bb, r, :] + bg_ref[...]
    bsum = _cumsum_rows(_log_sigmoid(gates), tri)
    lane = lax.broadcasted_iota(jnp.int32, gates.shape, 1)
    rows = jnp.where(lane < heads, gates, bsum).T

    for h in range(heads):
        i_col = gates[:, h:h + 1]
        b_col = bsum[:, heads + h:heads + h + 1]
        i_row = rows[h:h + 1, :]
        b_row = rows[heads + h:heads + h + 1, :]
        m_prev = m_ref[bb, h:h + 1, 0:1]
        n_prev = n_ref[bb, h:h + 1, :]
        C_prev = C_ref[bb, h]

        q = q_ref[bb, r, h * dk:(h + 1) * dk] * (dk ** -0.5)
        k = k_ref[bb, r, h * dk:(h + 1) * dk]
        vb = v_ref[bb, r, h * dv:(h + 1) * dv].astype(BF16)
        qb = q.astype(BF16)

        D = jnp.where(causal, (b_col - b_row) + i_row, NEG_BIG)
        inter = b_col + m_prev
        mt = jnp.maximum(inter, jnp.max(D, axis=-1, keepdims=True))
        a_inter = jnp.exp(inter - mt)
        S = lax.dot_general(qb, k.astype(BF16), (((1,), (1,)), ((), ())),
                            preferred_element_type=F32) * jnp.exp(D - mt)
        num = a_inter * jnp.dot(qb, C_prev.astype(BF16), preferred_element_type=F32) \
            + jnp.dot(S.astype(BF16), vb, preferred_element_type=F32)
        den = a_inter * jnp.sum(q * n_prev, axis=-1, keepdims=True) + jnp.sum(S, axis=-1, keepdims=True)
        hh = num / jnp.maximum(jnp.abs(den), jnp.exp(-mt))

        b_last = b_col[L - 1:L, :]
        wlast = (b_last - b_col) + i_col
        m_new = jnp.maximum(b_last + m_prev, jnp.max(wlast, axis=0, keepdims=True))
        a_c = jnp.exp((b_last + m_prev) - m_new)
        kw = k * jnp.exp(wlast - m_new)
        C_ref[bb, h] = a_c * C_prev + lax.dot_general(kw.astype(BF16), vb, (((0,), (0,)), ((), ())),
                                                     preferred_element_type=F32)
        n_ref[bb, h:h + 1, :] = a_c * n_prev + jnp.sum(kw, axis=0, keepdims=True)
        m_ref[bb, h:h + 1, :] = jnp.broadcast_to(m_new, (1, LANES))

        ms = jnp.mean(hh * hh, axis=-1, keepdims=True)
        y = (hh * lax.rsqrt(ms + EPS)) * mg_ref[:, h * dv:(h + 1) * dv]
        y = y * _sigmoid(o_ref[bb, r, h * dv:(h + 1) * dv])
        hm_ref[bb, r, h * dv:(h + 1) * dv] = y.astype(BF16)


def _mlstm(z, zg, b_gate, norm_g, C0, n0, m0, sl):
    B, T, _ = z.shape
    _, _, heads, dk, dv = C0.shape
    QW, VW = heads * dk, heads * dv
    assert VW == 2 * QW and 2 * heads <= LANES
    L = MLSTM_CHUNK if T % MLSTM_CHUNK == 0 else T
    Ls = MLSTM_STEP if (T % MLSTM_STEP == 0 and MLSTM_STEP % L == 0) else L
    BB = MLSTM_BATCH if B % MLSTM_BATCH == 0 else 1
    bias = jnp.zeros((1, LANES), F32).at[0, :2 * heads].set(b_gate)
    m0b = jnp.broadcast_to(m0[..., None], m0.shape + (LANES,))
    vmem = BB * (2 * (Ls * (2 * QW + 2 * VW + LANES) * 4 + Ls * VW * 2) + 4 * heads * dk * dv * 4) + 12 * MIB
    kern = functools.partial(_mlstm_kernel, heads=heads, dk=dk, dv=dv, chunk=L, nsub=Ls // L)
    hm, C, n, m = pl.pallas_call(
        kern,
        grid=(B // BB, T // Ls),
        in_specs=[
            pl.BlockSpec((BB, Ls, QW), lambda b, c: (b, c, 0)),
            pl.BlockSpec((BB, Ls, QW), lambda b, c: (b, c, 1)),
            pl.BlockSpec((BB, Ls, VW), lambda b, c: (b, c, 1)),
            pl.BlockSpec((BB, Ls, VW), lambda b, c: (b, c, 2)),
            pl.BlockSpec((BB, Ls, LANES), lambda b, c: (b, c, 0)),
            pl.BlockSpec((1, LANES), lambda b, c: (0, 0)),
            pl.BlockSpec((1, VW), lambda b, c: (0, 0)),
            pl.BlockSpec((1, BB, heads, dk, dv), lambda b, c: (sl, b, 0, 0, 0)),
            pl.BlockSpec((1, BB, heads, dk), lambda b, c: (sl, b, 0, 0)),
            pl.BlockSpec((1, BB, heads, LANES), lambda b, c: (sl, b, 0, 0)),
        ],
        out_specs=[
            pl.BlockSpec((BB, Ls, VW), lambda b, c: (b, c, 0)),
            pl.BlockSpec((BB, heads, dk, dv), lambda b, c: (b, 0, 0, 0)),
            pl.BlockSpec((BB, heads, dk), lambda b, c: (b, 0, 0)),
            pl.BlockSpec((BB, heads, LANES), lambda b, c: (b, 0, 0)),
        ],
        out_shape=[
            jax.ShapeDtypeStruct((B, T, VW), BF16),
            jax.ShapeDtypeStruct((B, heads, dk, dv), F32),
            jax.ShapeDtypeStruct((B, heads, dk), F32),
            jax.ShapeDtypeStruct((B, heads, LANES), F32),
        ],
        compiler_params=_params(("parallel", "arbitrary"), vmem),
        name="mlstm",
    )(z, z, z, z, zg, bias, norm_g.reshape(1, VW), C0, n0, m0b)
    return hm, C, n, m[:, :, 0]


def _hgrn_tables(nb):
    u = np.arange(nb)[:, None]
    j = np.arange(nb)[None, :]
    mats, masks = [j <= u], []
    w = nb // 2
    while w >= 1:
        r = (u // (2 * w)) * (2 * w) + w - 1
        upper = (u % (2 * w)) >= w
        mats.append(np.where(upper, (j > r) & (j <= u), (j > u) & (j <= r)))
        masks.append(((u // (2 * w)) == (j // (2 * w))) & upper & ((j % (2 * w)) < w))
        w //= 2
    mats.append(j > u)
    masks.append(u == j)
    return np.concatenate(mats, 0).astype(np.float32), np.stack(masks).astype(np.float32)


def _hgrn_kernel(*refs, heads, dk, dv, layer, chunk, ncast):
    q_ref, f_ref, i_ref, g_ref, lb_ref, ng_ref, S0_ref, mst_ref, lvl_ref = refs[:9]
    oh_ref, S_ref = refs[9 + ncast:11 + ncast]
    st_scr = refs[-1]
    nb = HGRN_BLOCK
    nlev = lvl_ref.shape[0]
    c = pl.program_id(1)

    for src, dst in zip(refs[9:9 + ncast], refs[11 + ncast:11 + 2 * ncast]):
        dst[...] = src[...].astype(BF16)

    @pl.when(c == 0)
    def _():
        for h in range(heads):
            st_scr[h] = S0_ref[0, 0, h].T

    lbs = lb_ref[...]
    e = jnp.exp(lbs - jnp.max(lbs, axis=0, keepdims=True))
    sm = e / jnp.sum(e, axis=0, keepdims=True)
    cum = sm[0:1, :]
    for j in range(1, layer + 1):
        cum = cum + sm[j:j + 1, :]
    lb = cum - sm[0:1, :]
    lb_floor = jnp.maximum(lb, LB_FLOOR)
    one_m = 1.0 - lb

    mst = mst_ref[...]
    masks = [lvl_ref[i] > 0.5 for i in range(nlev)]
    nt = (((1,), (1,)), ((), ()))

    def block(j, carry):
        rows = pl.ds(j * nb, nb)
        fx = f_ref[0, rows, :]
        en = jnp.exp(-jnp.abs(fx))
        rc = 1.0 / (1.0 + en)
        er = en * rc
        pos = fx >= 0.0
        g = jnp.log(lb_floor + one_m * jnp.where(pos, rc, er))
        kk = one_m * jnp.where(pos, er, rc)
        qs = _silu(q_ref[0, rows, :])
        vb = i_ref[0, rows, :].astype(BF16)
        gate = _silu(g_ref[0, rows, :])

        ghi = g.astype(BF16)
        gmid = (g - ghi.astype(F32)).astype(BF16)
        E = jnp.exp(jnp.dot(mst, ghi, preferred_element_type=F32)
                    + jnp.dot(mst, gmid, preferred_element_type=F32))
        EG = E[:nb]
        qG = (qs * EG).astype(BF16)
        kL = (kk * E[nlev * nb:]).astype(BF16)
        Ql = [(qs * E[(i + 1) * nb:(i + 2) * nb]).astype(BF16) for i in range(nlev - 1)] + [qs.astype(BF16)]
        Kl = [(kk * E[(i + 1) * nb:(i + 2) * nb]).astype(BF16) for i in range(nlev - 1)] + [kk.astype(BF16)]
        decay = EG[nb - 1:nb]

        for h in range(heads):
            sl = slice(h * dk, (h + 1) * dk)
            st = st_scr[h]
            o = lax.dot_general(qG[:, sl], st.astype(BF16), nt, preferred_element_type=F32)
            A = jnp.zeros((nb, nb), F32)
            for i in range(nlev):
                A = jnp.where(masks[i], lax.dot_general(Ql[i][:, sl], Kl[i][:, sl], nt,
                                                        preferred_element_type=F32), A)
            o = o + jnp.dot(A.astype(BF16), vb[:, sl], preferred_element_type=F32)
            upd = lax.dot_general(vb[:, sl], kL[:, sl], (((0,), (0,)), ((), ())),
                                  preferred_element_type=F32)
            st_scr[h] = st * decay[:, sl] + upd

            ms = jnp.mean(o * o, axis=-1, keepdims=True)
            y = (o * lax.rsqrt(ms + EPS)) * ng_ref[:, sl] * gate[:, sl]
            oh_ref[0, rows, sl] = y.astype(BF16)
        return carry

    for j in range(chunk // nb):
        block(j, 0)

    @pl.when(c == pl.num_programs(1) - 1)
    def _():
        for h in range(heads):
            S_ref[0, h] = st_scr[h].T


def _hgrn_steps(B, T):
    Lc = HGRN_CHUNK if T % HGRN_CHUNK == 0 else T
    return Lc, B * (T // Lc)


def _hgrn_can_cast(B, T, weights):
    _, steps = _hgrn_steps(B, T)
    return all(w.shape[1] % (steps * BF16_ROWS) == 0 for w in weights)


def _hgrn(z, lower_bounds, layer, norm_g, S0, sl, col_block, cast=()):
    B, T, _ = z.shape
    _, _, heads, dk, dv = S0.shape
    HW = heads * dk
    assert dk == dv
    Lc, steps = _hgrn_steps(B, T)
    NC = T // Lc
    assert Lc % HGRN_BLOCK == 0
    depth = lower_bounds.shape[0]
    mst, lvl = _hgrn_tables(HGRN_BLOCK)
    mst = jnp.asarray(mst, BF16)
    lvl = jnp.asarray(lvl, F32)
    slabs = [(w.shape[1] // steps, w.shape[2]) for w in cast]
    vmem = 2 * (4 * Lc * HW * 4 + Lc * HW * 2) + 5 * heads * dk * dv * 4 \
        + 6 * mst.shape[0] * HW * 4 + sum(2 * rb * cols * 6 for rb, cols in slabs) + 16 * MIB
    kern = functools.partial(_hgrn_kernel, heads=heads, dk=dk, dv=dv, layer=layer, chunk=Lc, ncast=len(cast))
    zspec = lambda idx: pl.BlockSpec((1, Lc, HW), lambda b, c: (b, c, idx))
    outs = pl.pallas_call(
        kern,
        grid=(B, NC),
        in_specs=[
            zspec(col_block), zspec(col_block + 1), zspec(col_block + 2), zspec(col_block + 3),
            pl.BlockSpec((depth, HW), lambda b, c: (0, 0)),
            pl.BlockSpec((1, HW), lambda b, c: (0, 0)),
            pl.BlockSpec((1, 1, heads, dk, dv), lambda b, c: (sl, b, 0, 0, 0)),
            pl.BlockSpec(mst.shape, lambda b, c: (0, 0)),
            pl.BlockSpec(lvl.shape, lambda b, c: (0, 0, 0)),
        ] + [pl.BlockSpec((None, rb, cols), lambda b, c: (layer, b * NC + c, 0)) for rb, cols in slabs],
        out_specs=[
            pl.BlockSpec((1, Lc, HW), lambda b, c: (b, c, 0)),
            pl.BlockSpec((1, heads, dk, dv), lambda b, c: (b, 0, 0, 0)),
        ] + [pl.BlockSpec((rb, cols), lambda b, c: (b * NC + c, 0)) for rb, cols in slabs],
        out_shape=[
            jax.ShapeDtypeStruct((B, T, HW), BF16),
            jax.ShapeDtypeStruct((B, heads, dk, dv), F32),
        ] + [jax.ShapeDtypeStruct(w.shape[1:], BF16) for w in cast],
        scratch_shapes=[pltpu.VMEM((heads, dv, dk), F32)],
        compiler_params=_params(("parallel", "arbitrary"), vmem),
        name="hgrn",
    )(z, z, z, z, lower_bounds, norm_g.reshape(1, HW), S0, mst, lvl, *cast)
    return outs[0], outs[1], list(outs[2:])


def _out_kernel(hm_ref, oh_ref, wa_ref, wb_ref, x_ref, ga_ref, xo_ref):
    acc = jnp.dot(hm_ref[...], wa_ref[...], preferred_element_type=F32)
    acc = acc + jnp.dot(oh_ref[...], wb_ref[...], preferred_element_type=F32)
    xo_ref[...] = x_ref[...] + ga_ref[...] * acc.reshape(xo_ref.shape)


def _out_proj(hm, oh, w_out, layer, x, mod, row0, *, tm=1024, tn=1024):
    B, T, D = x.shape
    MW, HW = hm.shape[-1], oh.shape[-1]
    assert MW == HW and w_out.shape[-2] == MW + HW
    G, R, TG, reps = _row_groups(B, T, tm)
    TM, M = TG * R, B * T
    tn = _col_tile(D, tn)
    vmem = 2 * (2 * TM * MW * 2 + 2 * MW * tn * 2 + 2 * TM * tn * 4 + tn * 4) + 2 * TM * tn * 4 + 4 * MIB
    return pl.pallas_call(
        _out_kernel,
        grid=(G // TG, D // tn),
        in_specs=[
            pl.BlockSpec((TM, MW), lambda i, j: (i, 0)),
            pl.BlockSpec((TM, HW), lambda i, j: (i, 0)),
            _weight_spec(w_out, layer, (MW, tn), lambda i, j: (0, j)),
            _weight_spec(w_out, layer, (HW, tn), lambda i, j: (1, j)),
            pl.BlockSpec((TG, R, tn), lambda i, j: (i, 0, j)),
            _mod_spec(mod, layer, row0, 2, TG, reps, tn, once=False),
        ],
        out_specs=pl.BlockSpec((TG, R, tn), lambda i, j: (i, 0, j)),
        out_shape=jax.ShapeDtypeStruct((G, R, D), F32),
        compiler_params=_params(("parallel", "arbitrary"), vmem),
        name="out_proj",
    )(hm.reshape(M, MW), oh.reshape(M, HW), w_out, w_out, x.reshape(G, R, D), mod).reshape(B, T, D)


def _ffn_kernel(x_ref, sc_ref, sh_ref, g_ref, ga_ref, wu_ref, wd_ref, fg_ref, o_ref, h_scr,
                *, final, n_split, f_split):
    f = pl.program_id(1)
    D = o_ref.shape[-1]
    dn = D // n_split
    ts = wu_ref.shape[-1] // f_split

    @pl.when(f == 0)
    def _():
        def piece(gs, rs, fs):
            h = _mod_norm(x_ref[gs, rs, :], g_ref[...], sc_ref[gs], sh_ref[gs])
            h_scr[fs, :] = h.reshape(-1, D).astype(BF16)
            o_ref[gs, rs, :] = jnp.zeros_like(h)

        _for_row_chunks(*x_ref.shape[:2], piece)

    blk = o_ref.shape[:-1] + (dn,)
    for k in range(f_split):
        u = jnp.dot(h_scr[...], wu_ref[:, k * ts:(k + 1) * ts], preferred_element_type=F32)
        a = jnp.square(jnp.maximum(u, 0.0)).astype(BF16)
        for n in range(n_split):
            sl = slice(n * dn, (n + 1) * dn)
            o_ref[:, :, sl] += jnp.dot(a, wd_ref[k * ts:(k + 1) * ts, sl], preferred_element_type=F32).reshape(blk)

    @pl.when(f == pl.num_programs(1) - 1)
    def _():
        def piece(gs, rs, fs):
            xo = x_ref[gs, rs, :] + ga_ref[gs] * o_ref[gs, rs, :]
            if final:
                ms = jnp.mean(xo * xo, axis=-1, keepdims=True)
                xo = (xo * lax.rsqrt(ms + EPS)) * fg_ref[...]
            o_ref[gs, rs, :] = xo

        _for_row_chunks(*x_ref.shape[:2], piece)


def _ffn(x, mod, row0, g, w_up, w_down, layer, final_g, *, final, tm=512, tf=1024):
    B, T, D = x.shape
    FF = w_up.shape[-1]
    G, R, TG, reps = _row_groups(B, T, tm)
    TM = TG * R
    tf = _col_tile(FF, tf)
    n_split = 4 if D % (4 * LANES) == 0 else 1
    f_split = 2 if tf % (2 * LANES) == 0 else 1
    vmem = 2 * TM * D * 4 + 2 * 2 * D * tf * 2 + (3 * TG + 4) * SUBLANES * D * 4 + TM * D * 2 + TM * tf * 6 \
        + 2 * TM * (D // n_split) * 4 + 4 * MIB
    kern = functools.partial(_ffn_kernel, final=final, n_split=n_split, f_split=f_split)
    once = pl.Buffered(1)
    vec = lambda part: _mod_spec(mod, layer, row0, part, TG, reps, D)
    return pl.pallas_call(
        kern,
        grid=(G // TG, FF // tf),
        in_specs=[
            pl.BlockSpec((TG, R, D), lambda i, f: (i, 0, 0), pipeline_mode=once),
            vec(4), vec(3),
            pl.BlockSpec((None, 1, D), lambda i, f: (layer, 0, 0), pipeline_mode=once),
            vec(5),
            _weight_spec(w_up, layer, (D, tf), lambda i, f: (0, f)),
            _weight_spec(w_down, layer, (tf, D), lambda i, f: (f, 0)),
            pl.BlockSpec((1, D), lambda i, f: (0, 0), pipeline_mode=once),
        ],
        out_specs=pl.BlockSpec((TG, R, D), lambda i, f: (i, 0, 0), pipeline_mode=once),
        out_shape=jax.ShapeDtypeStruct((G, R, D), F32),
        scratch_shapes=[pltpu.VMEM((TM, D), BF16)],
        compiler_params=_params(("parallel", "arbitrary"), vmem),
        name="ffn",
    )(x.reshape(G, R, D), mod, mod, g.reshape(-1, 1, D), mod, w_up, w_down,
      final_g.reshape(1, D)).reshape(B, T, D)


def _trunk(x, mod, row0, C0, n0, m0, S0, per_layer_state, wts, late=None):
    depth = mod.shape[0]
    made = []
    Cs, ns, ms, Ss = [], [], [], []
    for l in range(depth):
        sl = l if per_layer_state else 0
        z, zg = _in_proj(x, mod, row0, wts["norm1_g"], wts["w_main"], wts["w_gate"], l)
        hm, C, n, m = _mlstm(z, zg, wts["b_gate"][l], wts["mlstm_norm_g"][l], C0, n0, m0, sl)
        oh, S, cast = _hgrn(z, wts["lower_bounds"], l, wts["hgrn_norm_g"][l], S0, sl, col_block=3,
                            cast=wts["late_f32"] if late is None else ())
        w_out, w_up, w_down = cast if late is None else late[l]
        made.append((w_out, w_up, w_down))
        x = _out_proj(hm, oh, w_out, l, x, mod, row0)
        x = _ffn(x, mod, row0, wts["norm2_g"], w_up, w_down, l, wts["final_g"], final=(l == depth - 1))
        Cs.append(C); ns.append(n); ms.append(m); Ss.append(S)
    return x, jnp.stack(Cs), jnp.stack(ns), jnp.stack(ms), jnp.stack(Ss), made


def kernel(x_prompt, x_sample, state_mlstm_C, state_mlstm_n, state_mlstm_m, state_hgrn_S, c_prompt, c_sample, w_mod, b_mod, norm1_g, w_in, b_gate, lower_bounds, mlstm_norm_g, hgrn_norm_g, w_out, norm2_g, w_up, w_down, final_g):
    depth = w_mod.shape[0]
    Bp, Bs = x_prompt.shape[0], x_sample.shape[0]
    _, _, heads, dk, dv = state_mlstm_C.shape
    _, _, hh, hdk, hdv = state_hgrn_S.shape
    gate0 = 2 * heads * dk + 2 * heads * dv
    ngate = 2 * heads

    w_main = jnp.concatenate([w_in[:, :, :gate0], w_in[:, :, gate0 + ngate:]], axis=-1).astype(BF16)
    w_gate = jnp.pad(w_in[:, :, gate0:gate0 + ngate], ((0, 0), (0, 0), (0, LANES - ngate))).astype(BF16)
    wts = dict(norm1_g=norm1_g, w_main=w_main, w_gate=w_gate, b_gate=b_gate, lower_bounds=lower_bounds,
               mlstm_norm_g=mlstm_norm_g, hgrn_norm_g=hgrn_norm_g, norm2_g=norm2_g, final_g=final_g,
               late_f32=(w_out, w_up, w_down))
    late = None
    if not _hgrn_can_cast(*x_prompt.shape[:2], wts["late_f32"]):
        late = [(w_out.astype(BF16), w_up.astype(BF16), w_down.astype(BF16))] * depth

    pad_rows = lambda c: jnp.pad(c, ((0, -c.shape[0] % BF16_ROWS), (0, 0)))
    c_all = jnp.concatenate([pad_rows(c_prompt), pad_rows(c_sample)], axis=0)
    row_s = Bp + (-Bp % BF16_ROWS)
    mod = _modulation(c_all, w_mod, b_mod)
    mod = mod.reshape(depth, c_all.shape[0], 1, mod.shape[-1])

    zC = jnp.zeros((1, Bp, heads, dk, dv), F32)
    zn = jnp.zeros((1, Bp, heads, dk), F32)
    zm = jnp.zeros((1, Bp, heads), F32)
    zS = jnp.zeros((1, Bp, hh, hdk, hdv), F32)
    y_p, pC, pn, pm, pS, late = _trunk(x_prompt, mod, 0, zC, zn, zm, zS, False, wts, late)
    y_s, sC, sn, sm, sS, _ = _trunk(x_sample, mod, row_s, state_mlstm_C, state_mlstm_n, state_mlstm_m,
                                    state_hgrn_S, True, wts, late)
    return (y_p, y_s, pC, pn, pm, pS, sC, sn, sm, sS)
```

```python
import functools

import jax
import jax.numpy as jnp
import numpy as np
from jax import lax
from jax.experimental import pallas as pl
from jax.experimental.pallas import tpu as pltpu

EPS = 1e-6
NEG_BIG = -1e30
LB_FLOOR = 1e-30
MLSTM_CHUNK = 256
MLSTM_STEP = 512
MLSTM_BATCH = 1
HGRN_BLOCK = 16
HGRN_CHUNK = 128
LANES = 128
SUBLANES = 8
BF16_ROWS = 16
MIB = 1024 * 1024
VMEM_CAP = 63 * MIB

F32 = jnp.float32
BF16 = jnp.bfloat16


def _params(semantics, vmem_bytes):
    return pltpu.CompilerParams(dimension_semantics=semantics,
                                vmem_limit_bytes=int(min(VMEM_CAP, vmem_bytes)))


def _sigmoid(x):
    return 1.0 / (1.0 + jnp.exp(-x))


def _silu(x):
    return x * _sigmoid(x)


def _log_sigmoid(x):
    return jnp.minimum(x, 0.0) - jnp.log1p(jnp.exp(-jnp.abs(x)))


def _lower_tri(n, dtype):
    r = lax.broadcasted_iota(jnp.int32, (n, n), 0)
    c = lax.broadcasted_iota(jnp.int32, (n, n), 1)
    return jnp.where(r >= c, 1.0, 0.0).astype(dtype)


def _cumsum_rows(x, tri):
    hi = x.astype(BF16)
    r1 = x - hi.astype(F32)
    mid = r1.astype(BF16)
    lo = (r1 - mid.astype(F32)).astype(BF16)
    dot = functools.partial(jnp.dot, preferred_element_type=F32)
    return (dot(tri, hi) + dot(tri, mid)) + dot(tri, lo)


def _col_tile(n, pref):
    return max(t for t in range(LANES, min(pref, n) + 1, LANES) if n % t == 0)


def _row_groups(B, T, tm_pref):
    if T >= tm_pref:
        assert T % tm_pref == 0
        R, TG = tm_pref, 1
    else:
        R = T
        TG = max(1, min(B, tm_pref // T))
        assert B % TG == 0
    return (B * T) // R, R, TG, T // R


def _for_row_chunks(TG, R, body, step=4 * BF16_ROWS):
    step = step if R % step == 0 else R
    per = R // step

    def piece(c, carry):
        t = c // per
        r = pl.multiple_of((c % per) * step, step)
        body(pl.ds(t, 1), pl.ds(r, step), pl.ds(pl.multiple_of(t * R + r, step), step))
        return carry

    lax.fori_loop(0, TG * per, piece, 0)


def _mod_spec(mod, layer, row0, part, TG, reps, width, once=True):
    D = mod.shape[-1] // 6
    per = D // width
    mode = dict(pipeline_mode=pl.Buffered(1)) if once else {}
    col = (lambda j: part) if per == 1 else (lambda j: part * per + j)
    if reps > 1:
        assert TG == 1
        return pl.BlockSpec((None, 1, 1, width), lambda i, j: (layer, row0 + i // reps, 0, col(j)), **mode)
    assert row0 % TG == 0
    return pl.BlockSpec((None, TG, 1, width), lambda i, j: (layer, row0 // TG + i, 0, col(j)), **mode)


def _weight_spec(w, layer, block, index):
    if w.ndim == 2:
        return pl.BlockSpec(block, index)
    return pl.BlockSpec((None,) + block, lambda i, j: (layer,) + index(i, j))


def _mod_norm(x, g, sc, sh):
    ms = jnp.mean(x * x, axis=-1, keepdims=True)
    return (x * lax.rsqrt(ms + EPS) * g) * (1.0 + sc) + sh


def _mod_kernel(c_ref, w_ref, b_ref, o_ref):
    c = c_ref[...]
    a = _silu(c).astype(BF16)
    w = w_ref[0].astype(BF16)
    o_ref[0] = jnp.dot(a, w, preferred_element_type=F32) + b_ref[0]


def _modulation(c_all, w_mod, b_mod, *, tn=512):
    depth, D, N = w_mod.shape
    MP = c_all.shape[0]
    tn = _col_tile(N, tn)
    vmem = 2 * (D * tn * 4 + MP * tn * 4 + tn * 4) + 2 * MP * D * 4 + D * tn * 2 + 4 * MIB
    return pl.pallas_call(
        _mod_kernel,
        grid=(depth, N // tn),
        in_specs=[
            pl.BlockSpec((MP, D), lambda l, j: (0, 0)),
            pl.BlockSpec((1, D, tn), lambda l, j: (l, 0, j)),
            pl.BlockSpec((1, 1, tn), lambda l, j: (l, 0, j)),
        ],
        out_specs=pl.BlockSpec((1, MP, tn), lambda l, j: (l, 0, j)),
        out_shape=jax.ShapeDtypeStruct((depth, MP, N), F32),
        compiler_params=_params(("parallel", "parallel"), vmem),
        name="modulation",
    )(c_all, w_mod, b_mod.reshape(depth, 1, N))


def _win_kernel(w_ref, main_ref, gate_ref, *, gate0, ngate):
    w = w_ref[...]
    main_ref[:, :gate0] = w[:, :gate0].astype(BF16)
    main_ref[:, gate0:] = w[:, gate0 + ngate:].astype(BF16)
    win = w[:, gate0:gate0 + LANES]
    lane = lax.broadcasted_iota(jnp.int32, win.shape, 1)
    gate_ref[...] = jnp.where(lane < ngate, win, 0.0).astype(BF16)


def _prep_w_in(w_in, gate0, ngate, *, rows=128):
    depth, D, NC = w_in.shape
    NZ = NC - ngate
    assert gate0 % LANES == 0 and NZ % LANES == 0 and gate0 + LANES <= NC
    rows = rows if D % rows == 0 else D
    vmem = 2 * rows * (NC * 4 + NZ * 2 + LANES * 2) + 3 * rows * NC * 4 + 4 * MIB
    return pl.pallas_call(
        functools.partial(_win_kernel, gate0=gate0, ngate=ngate),
        grid=(depth, D // rows),
        in_specs=[pl.BlockSpec((None, rows, NC), lambda l, r: (l, r, 0))],
        out_specs=[
            pl.BlockSpec((None, rows, NZ), lambda l, r: (l, r, 0)),
            pl.BlockSpec((None, rows, LANES), lambda l, r: (l, r, 0)),
        ],
        out_shape=[jax.ShapeDtypeStruct((depth, D, NZ), BF16), jax.ShapeDtypeStruct((depth, D, LANES), BF16)],
        compiler_params=_params(("parallel", "parallel"), vmem),
        name="w_in_prep",
    )(w_in)


def _in_kernel(x_ref, sc_ref, sh_ref, g_ref, w_ref, wg_ref, z_ref, zg_ref, h_scr):
    @pl.when(pl.program_id(1) == 0)
    def _():
        def piece(gs, rs, fs):
            h = _mod_norm(x_ref[gs, rs, :], g_ref[...], sc_ref[gs], sh_ref[gs])
            hb = h.reshape(-1, h.shape[-1]).astype(BF16)
            h_scr[fs, :] = hb
            zg_ref[fs, :] = jnp.dot(hb, wg_ref[...], preferred_element_type=F32)

        _for_row_chunks(*x_ref.shape[:2], piece)

    z_ref[...] = jnp.dot(h_scr[...], w_ref[...], preferred_element_type=F32)


def _in_proj(x, mod, row0, g, w_main, w_gate, layer, *, tm=512, tn=1024):
    B, T, D = x.shape
    NZ = w_main.shape[-1]
    G, R, TG, reps = _row_groups(B, T, tm)
    TM, M = TG * R, B * T
    tn = _col_tile(NZ, tn)
    vmem = 2 * TM * D * 4 + D * LANES * 2 + (2 * TG + 1) * SUBLANES * D * 4 \
        + 2 * (D * tn * 2 + TM * tn * 4 + TM * LANES * 4) + TM * D * 2 + 8 * MIB
    once = pl.Buffered(1)
    z, zg = pl.pallas_call(
        _in_kernel,
        grid=(G // TG, NZ // tn),
        in_specs=[
            pl.BlockSpec((TG, R, D), lambda i, j: (i, 0, 0)),
            _mod_spec(mod, layer, row0, 1, TG, reps, D),
            _mod_spec(mod, layer, row0, 0, TG, reps, D),
            pl.BlockSpec((None, 1, D), lambda i, j: (layer, 0, 0), pipeline_mode=once),
            pl.BlockSpec((None, D, tn), lambda i, j: (layer, 0, j)),
            pl.BlockSpec((None, D, LANES), lambda i, j: (layer, 0, 0), pipeline_mode=once),
        ],
        out_specs=[
            pl.BlockSpec((TM, tn), lambda i, j: (i, j)),
            pl.BlockSpec((TM, LANES), lambda i, j: (i, 0)),
        ],
        out_shape=[jax.ShapeDtypeStruct((M, NZ), F32), jax.ShapeDtypeStruct((M, LANES), F32)],
        scratch_shapes=[pltpu.VMEM((TM, D), BF16)],
        compiler_params=_params(("parallel", "arbitrary"), vmem),
        name="in_proj",
    )(x.reshape(G, R, D), mod, mod, g.reshape(-1, 1, D), w_main, w_gate)
    return z.reshape(B, T, NZ), zg.reshape(B, T, LANES)


def _mlstm_kernel(q_ref, k_ref, v_ref, o_ref, zg_ref, bg_ref, mg_ref, C0_ref, n0_ref, m0_ref,
                  hm_ref, C_ref, n_ref, m_ref, *, heads, dk, dv, chunk, nsub):
    L = chunk

    @pl.when(pl.program_id(1) == 0)
    def _():
        C_ref[...] = C0_ref[0]
        n_ref[...] = n0_ref[0]
        m_ref[...] = m0_ref[0]

    tri = _lower_tri(L, BF16)
    causal = (lax.broadcasted_iota(jnp.int32, (L, L), 0) >= lax.broadcasted_iota(jnp.int32, (L, L), 1))
    for s in range(nsub):
        for bb in range(q_ref.shape[0]):
            _mlstm_chunk(bb, slice(s * L, (s + 1) * L), tri, causal, q_ref, k_ref, v_ref, o_ref, zg_ref, bg_ref,
                         mg_ref, hm_ref, C_ref, n_ref, m_ref, heads=heads, dk=dk, dv=dv, L=L)


def _mlstm_chunk(bb, r, tri, causal, q_ref, k_ref, v_ref, o_ref, zg_ref, bg_ref, mg_ref,
                 hm_ref, C_ref, n_ref, m_ref, *, heads, dk, dv, L):
    gates = zg_ref[bb, r, :] + bg_ref[...]
    bsum = _cumsum_rows(_log_sigmoid(gates), tri)
    lane = lax.broadcasted_iota(jnp.int32, gates.shape, 1)
    rows = jnp.where(lane < heads, gates, bsum).T

    for h in range(heads):
        i_col = gates[:, h:h + 1]
        b_col = bsum[:, heads + h:heads + h + 1]
        i_row = rows[h:h + 1, :]
        b_row = rows[heads + h:heads + h + 1, :]
        m_prev = m_ref[bb, h:h + 1, 0:1]
        n_prev = n_ref[bb, h:h + 1, :]
        C_prev = C_ref[bb, h]

        q = q_ref[bb, r, h * dk:(h + 1) * dk] * (dk ** -0.5)
        k = k_ref[bb, r, h * dk:(h + 1) * dk]
        vb = v_ref[bb, r, h * dv:(h + 1) * dv].astype(BF16)
        qb = q.astype(BF16)

        D = jnp.where(causal, (b_col - b_row) + i_row, NEG_BIG)
        inter = b_col + m_prev
        mt = jnp.maximum(inter, jnp.max(D, axis=-1, keepdims=True))
        a_inter = jnp.exp(inter - mt)
        S = lax.dot_general(qb, k.astype(BF16), (((1,), (1,)), ((), ())),
                            preferred_element_type=F32) * jnp.exp(D - mt)
        num = a_inter * jnp.dot(qb, C_prev.astype(BF16), preferred_element_type=F32) \
            + jnp.dot(S.astype(BF16), vb, preferred_element_type=F32)
        den = a_inter * jnp.sum(q * n_prev, axis=-1, keepdims=True) + jnp.sum(S, axis=-1, keepdims=True)
        hh = num / jnp.maximum(jnp.abs(den), jnp.exp(-mt))

        b_last = b_col[L - 1:L, :]
        wlast = (b_last - b_col) + i_col
        m_new = jnp.maximum(b_last + m_prev, jnp.max(wlast, axis=0, keepdims=True))
        a_c = jnp.exp((b_last + m_prev) - m_new)
        kw = k * jnp.exp(wlast - m_new)
        C_ref[bb, h] = a_c * C_prev + lax.dot_general(kw.astype(BF16), vb, (((0,), (0,)), ((), ())),
                                                     preferred_element_type=F32)
        n_ref[bb, h:h + 1, :] = a_c * n_prev + jnp.sum(kw, axis=0, keepdims=True)
        m_ref[bb, h:h + 1, :] = jnp.broadcast_to(m_new, (1, LANES))

        ms = jnp.mean(hh * hh, axis=-1, keepdims=True)
        y = (hh * lax.rsqrt(ms + EPS)) * mg_ref[:, h * dv:(h + 1) * dv]
        y = y * _sigmoid(o_ref[bb, r, h * dv:(h + 1) * dv])
        hm_ref[bb, r, h * dv:(h + 1) * dv] = y.astype(BF16)


def _mlstm(z, zg, b_gate, norm_g, C0, n0, m0, sl):
    B, T, _ = z.shape
    _, _, heads, dk, dv = C0.shape
    QW, VW = heads * dk, heads * dv
    assert VW == 2 * QW and 2 * heads <= LANES
    L = MLSTM_CHUNK if T % MLSTM_CHUNK == 0 else T
    Ls = MLSTM_STEP if (T % MLSTM_STEP == 0 and MLSTM_STEP % L == 0) else L
    BB = MLSTM_BATCH if B % MLSTM_BATCH == 0 else 1
    bias = jnp.zeros((1, LANES), F32).at[0, :2 * heads].set(b_gate)
    m0b = jnp.broadcast_to(m0[..., None], m0.shape + (LANES,))
    vmem = BB * (2 * (Ls * (2 * QW + 2 * VW + LANES) * 4 + Ls * VW * 2) + 4 * heads * dk * dv * 4) + 12 * MIB
    kern = functools.partial(_mlstm_kernel, heads=heads, dk=dk, dv=dv, chunk=L, nsub=Ls // L)
    hm, C, n, m = pl.pallas_call(
        kern,
        grid=(B // BB, T // Ls),
        in_specs=[
            pl.BlockSpec((BB, Ls, QW), lambda b, c: (b, c, 0)),
            pl.BlockSpec((BB, Ls, QW), lambda b, c: (b, c, 1)),
            pl.BlockSpec((BB, Ls, VW), lambda b, c: (b, c, 1)),
            pl.BlockSpec((BB, Ls, VW), lambda b, c: (b, c, 2)),
            pl.BlockSpec((BB, Ls, LANES), lambda b, c: (b, c, 0)),
            pl.BlockSpec((1, LANES), lambda b, c: (0, 0)),
            pl.BlockSpec((1, VW), lambda b, c: (0, 0)),
            pl.BlockSpec((1, BB, heads, dk, dv), lambda b, c: (sl, b, 0, 0, 0)),
            pl.BlockSpec((1, BB, heads, dk), lambda b, c: (sl, b, 0, 0)),
            pl.BlockSpec((1, BB, heads, LANES), lambda b, c: (sl, b, 0, 0)),
        ],
        out_specs=[
            pl.BlockSpec((BB, Ls, VW), lambda b, c: (b, c, 0)),
            pl.BlockSpec((BB, heads, dk, dv), lambda b, c: (b, 0, 0, 0)),
            pl.BlockSpec((BB, heads, dk), lambda b, c: (b, 0, 0)),
            pl.BlockSpec((BB, heads, LANES), lambda b, c: (b, 0, 0)),
        ],
        out_shape=[
            jax.ShapeDtypeStruct((B, T, VW), BF16),
            jax.ShapeDtypeStruct((B, heads, dk, dv), F32),
            jax.ShapeDtypeStruct((B, heads, dk), F32),
            jax.ShapeDtypeStruct((B, heads, LANES), F32),
        ],
        compiler_params=_params(("parallel", "arbitrary"), vmem),
        name="mlstm",
    )(z, z, z, z, zg, bias, norm_g.reshape(1, VW), C0, n0, m0b)
    return hm, C, n, m[:, :, 0]


def _hgrn_tables(nb):
    u = np.arange(nb)[:, None]
    j = np.arange(nb)[None, :]
    mats, masks = [j <= u], []
    w = nb // 2
    while w >= 1:
        r = (u // (2 * w)) * (2 * w) + w - 1
        upper = (u % (2 * w)) >= w
        mats.append(np.where(upper, (j > r) & (j <= u), (j > u) & (j <= r)))
        masks.append(((u // (2 * w)) == (j // (2 * w))) & upper & ((j % (2 * w)) < w))
        w //= 2
    mats.append(j > u)
    masks.append(u == j)
    masks = np.stack(masks).astype(np.float32)
    return np.concatenate(mats, 0).astype(np.float32), np.concatenate([masks, masks], axis=2)


def _hgrn_kernel(*refs, heads, dk, dv, layer, chunk, ncast):
    q_ref, f_ref, i_ref, g_ref, lb_ref, ng_ref, S0_ref, mst_ref, lvl_ref = refs[:9]
    oh_ref, S_ref = refs[9 + ncast:11 + ncast]
    st_scr = refs[-1]
    nb = HGRN_BLOCK
    nlev = lvl_ref.shape[0]
    c = pl.program_id(1)

    for src, dst in zip(refs[9:9 + ncast], refs[11 + ncast:11 + 2 * ncast]):
        dst[...] = src[...].astype(BF16)

    @pl.when(c == 0)
    def _():
        for h in range(heads):
            st_scr[h] = S0_ref[0, 0, h].T

    lbs = lb_ref[...]
    e = jnp.exp(lbs - jnp.max(lbs, axis=0, keepdims=True))
    sm = e / jnp.sum(e, axis=0, keepdims=True)
    cum = sm[0:1, :]
    for j in range(1, layer + 1):
        cum = cum + sm[j:j + 1, :]
    lb = cum - sm[0:1, :]
    lb_floor = jnp.maximum(lb, LB_FLOOR)
    one_m = 1.0 - lb

    mst = mst_ref[...]
    masks = [lvl_ref[i] > 0.5 for i in range(nlev)]
    nt = (((1,), (1,)), ((), ()))
    first = lax.broadcasted_iota(jnp.int32, (nb, 2 * dk), 1) < dk

    def pair_diag(a):
        zero = jnp.zeros_like(a)
        return jnp.concatenate([jnp.where(first, a, zero), jnp.where(first, zero, a)], axis=0)

    def block(j, carry):
        rows = pl.ds(j * nb, nb)
        fx = f_ref[0, rows, :]
        en = jnp.exp(-jnp.abs(fx))
        rc = 1.0 / (1.0 + en)
        er = en * rc
        pos = fx >= 0.0
        g = jnp.log(lb_floor + one_m * jnp.where(pos, rc, er))
        kk = one_m * jnp.where(pos, er, rc)
        qs = _silu(q_ref[0, rows, :])
        vb = i_ref[0, rows, :].astype(BF16)
        gate = _silu(g_ref[0, rows, :])

        ghi = g.astype(BF16)
        gmid = (g - ghi.astype(F32)).astype(BF16)
        E = jnp.exp(jnp.dot(mst, ghi, preferred_element_type=F32)
                    + jnp.dot(mst, gmid, preferred_element_type=F32))
        EG = E[:nb]
        qG = (qs * EG).astype(BF16)
        kL = (kk * E[nlev * nb:]).astype(BF16)
        Ql = [(qs * E[(i + 1) * nb:(i + 2) * nb]).astype(BF16) for i in range(nlev - 1)] + [qs.astype(BF16)]
        Kl = [(kk * E[(i + 1) * nb:(i + 2) * nb]).astype(BF16) for i in range(nlev - 1)] + [kk.astype(BF16)]
        decay = EG[nb - 1:nb]

        pairs = [slice(2 * p * dk, (2 * p + 2) * dk) for p in range(heads // 2)]
        A2 = []
        for sl2 in pairs:
            a = jnp.zeros((nb, 2 * nb), F32)
            for i in range(nlev):
                a = jnp.where(masks[i], lax.dot_general(Ql[i][:, sl2], pair_diag(Kl[i][:, sl2]), nt,
                                                        preferred_element_type=F32), a)
            A2.append(a.astype(BF16))
        o_intra = jnp.concatenate(
            [jnp.dot(a, pair_diag(vb[:, sl2]), preferred_element_type=F32) for a, sl2 in zip(A2, pairs)], axis=1)

        for h in range(heads):
            sl = slice(h * dk, (h + 1) * dk)
            st = st_scr[h]
            o = lax.dot_general(qG[:, sl], st.astype(BF16), nt, preferred_element_type=F32)
            o = o + o_intra[:, sl]
            upd = lax.dot_general(vb[:, sl], kL[:, sl], (((0,), (0,)), ((), ())),
                                  preferred_element_type=F32)
            st_scr[h] = st * decay[:, sl] + upd

            ms = jnp.mean(o * o, axis=-1, keepdims=True)
            y = (o * lax.rsqrt(ms + EPS)) * ng_ref[:, sl] * gate[:, sl]
            oh_ref[0, rows, sl] = y.astype(BF16)
        return carry

    for j in range(chunk // nb):
        block(j, 0)

    @pl.when(c == pl.num_programs(1) - 1)
    def _():
        for h in range(heads):
            S_ref[0, h] = st_scr[h].T


def _hgrn_steps(B, T):
    Lc = HGRN_CHUNK if T % HGRN_CHUNK == 0 else T
    return Lc, B * (T // Lc)


def _hgrn_can_cast(B, T, weights):
    _, steps = _hgrn_steps(B, T)
    return all(w.shape[1] % (steps * BF16_ROWS) == 0 for w in weights)


def _hgrn(z, lower_bounds, layer, norm_g, S0, sl, col_block, cast=()):
    B, T, _ = z.shape
    _, _, heads, dk, dv = S0.shape
    HW = heads * dk
    assert dk == dv
    Lc, steps = _hgrn_steps(B, T)
    NC = T // Lc
    assert Lc % HGRN_BLOCK == 0
    depth = lower_bounds.shape[0]
    mst, lvl = _hgrn_tables(HGRN_BLOCK)
    mst = jnp.asarray(mst, BF16)
    lvl = jnp.asarray(lvl, F32)
    slabs = [(w.shape[1] // steps, w.shape[2]) for w in cast]
    vmem = 2 * (4 * Lc * HW * 4 + Lc * HW * 2) + 5 * heads * dk * dv * 4 \
        + 6 * mst.shape[0] * HW * 4 + sum(2 * rb * cols * 6 for rb, cols in slabs) + 16 * MIB
    kern = functools.partial(_hgrn_kernel, heads=heads, dk=dk, dv=dv, layer=layer, chunk=Lc, ncast=len(cast))
    zspec = lambda idx: pl.BlockSpec((1, Lc, HW), lambda b, c: (b, c, idx))
    outs = pl.pallas_call(
        kern,
        grid=(B, NC),
        in_specs=[
            zspec(col_block), zspec(col_block + 1), zspec(col_block + 2), zspec(col_block + 3),
            pl.BlockSpec((depth, HW), lambda b, c: (0, 0)),
            pl.BlockSpec((1, HW), lambda b, c: (0, 0)),
            pl.BlockSpec((1, 1, heads, dk, dv), lambda b, c: (sl, b, 0, 0, 0)),
            pl.BlockSpec(mst.shape, lambda b, c: (0, 0)),
            pl.BlockSpec(lvl.shape, lambda b, c: (0, 0, 0)),
        ] + [pl.BlockSpec((None, rb, cols), lambda b, c: (layer, b * NC + c, 0)) for rb, cols in slabs],
        out_specs=[
            pl.BlockSpec((1, Lc, HW), lambda b, c: (b, c, 0)),
            pl.BlockSpec((1, heads, dk, dv), lambda b, c: (b, 0, 0, 0)),
        ] + [pl.BlockSpec((rb, cols), lambda b, c: (b * NC + c, 0)) for rb, cols in slabs],
        out_shape=[
            jax.ShapeDtypeStruct((B, T, HW), BF16),
            jax.ShapeDtypeStruct((B, heads, dk, dv), F32),
        ] + [jax.ShapeDtypeStruct(w.shape[1:], BF16) for w in cast],
        scratch_shapes=[pltpu.VMEM((heads, dv, dk), F32)],
        compiler_params=_params(("parallel", "arbitrary"), vmem),
        name="hgrn",
    )(z, z, z, z, lower_bounds, norm_g.reshape(1, HW), S0, mst, lvl, *cast)
    return outs[0], outs[1], list(outs[2:])


def _out_kernel(hm_ref, oh_ref, wa_ref, wb_ref, x_ref, ga_ref, xo_ref):
    acc = jnp.dot(hm_ref[...], wa_ref[...], preferred_element_type=F32)
    acc = acc + jnp.dot(oh_ref[...], wb_ref[...], preferred_element_type=F32)
    xo_ref[...] = x_ref[...] + ga_ref[...] * acc.reshape(xo_ref.shape)


def _out_proj(hm, oh, w_out, layer, x, mod, row0, *, tm=1024, tn=1024):
    B, T, D = x.shape
    MW, HW = hm.shape[-1], oh.shape[-1]
    assert MW == HW and w_out.shape[-2] == MW + HW
    G, R, TG, reps = _row_groups(B, T, tm)
    TM, M = TG * R, B * T
    tn = _col_tile(D, tn)
    vmem = 2 * (2 * TM * MW * 2 + 2 * MW * tn * 2 + 2 * TM * tn * 4 + tn * 4) + 2 * TM * tn * 4 + 4 * MIB
    return pl.pallas_call(
        _out_kernel,
        grid=(G // TG, D // tn),
        in_specs=[
            pl.BlockSpec((TM, MW), lambda i, j: (i, 0)),
            pl.BlockSpec((TM, HW), lambda i, j: (i, 0)),
            _weight_spec(w_out, layer, (MW, tn), lambda i, j: (0, j)),
            _weight_spec(w_out, layer, (HW, tn), lambda i, j: (1, j)),
            pl.BlockSpec((TG, R, tn), lambda i, j: (i, 0, j)),
            _mod_spec(mod, layer, row0, 2, TG, reps, tn, once=False),
        ],
        out_specs=pl.BlockSpec((TG, R, tn), lambda i, j: (i, 0, j)),
        out_shape=jax.ShapeDtypeStruct((G, R, D), F32),
        compiler_params=_params(("parallel", "arbitrary"), vmem),
        name="out_proj",
    )(hm.reshape(M, MW), oh.reshape(M, HW), w_out, w_out, x.reshape(G, R, D), mod).reshape(B, T, D)


def _ffn_kernel(x_ref, sc_ref, sh_ref, g_ref, ga_ref, wu_ref, wd_ref, fg_ref, o_ref, h_scr,
                *, final, n_split, f_split):
    f = pl.program_id(1)
    D = o_ref.shape[-1]
    dn = D // n_split
    ts = wu_ref.shape[-1] // f_split

    @pl.when(f == 0)
    def _():
        def piece(gs, rs, fs):
            h = _mod_norm(x_ref[gs, rs, :], g_ref[...], sc_ref[gs], sh_ref[gs])
            h_scr[fs, :] = h.reshape(-1, D).astype(BF16)
            o_ref[gs, rs, :] = jnp.zeros_like(h)

        _for_row_chunks(*x_ref.shape[:2], piece)

    blk = o_ref.shape[:-1] + (dn,)
    for k in range(f_split):
        u = jnp.dot(h_scr[...], wu_ref[:, k * ts:(k + 1) * ts], preferred_element_type=F32)
        a = jnp.square(jnp.maximum(u, 0.0)).astype(BF16)
        for n in range(n_split):
            sl = slice(n * dn, (n + 1) * dn)
            o_ref[:, :, sl] += jnp.dot(a, wd_ref[k * ts:(k + 1) * ts, sl], preferred_element_type=F32).reshape(blk)

    @pl.when(f == pl.num_programs(1) - 1)
    def _():
        def piece(gs, rs, fs):
            xo = x_ref[gs, rs, :] + ga_ref[gs] * o_ref[gs, rs, :]
            if final:
                ms = jnp.mean(xo * xo, axis=-1, keepdims=True)
                xo = (xo * lax.rsqrt(ms + EPS)) * fg_ref[...]
            o_ref[gs, rs, :] = xo

        _for_row_chunks(*x_ref.shape[:2], piece)


def _ffn(x, mod, row0, g, w_up, w_down, layer, final_g, *, final, tm=512, tf=1024):
    B, T, D = x.shape
    FF = w_up.shape[-1]
    G, R, TG, reps = _row_groups(B, T, tm)
    TM = TG * R
    tf = _col_tile(FF, tf)
    n_split = 4 if D % (4 * LANES) == 0 else 1
    f_split = 2 if tf % (2 * LANES) == 0 else 1
    vmem = 2 * TM * D * 4 + 2 * 2 * D * tf * 2 + (3 * TG + 4) * SUBLANES * D * 4 + TM * D * 2 + TM * tf * 6 \
        + 2 * TM * (D // n_split) * 4 + 4 * MIB
    kern = functools.partial(_ffn_kernel, final=final, n_split=n_split, f_split=f_split)
    once = pl.Buffered(1)
    vec = lambda part: _mod_spec(mod, layer, row0, part, TG, reps, D)
    return pl.pallas_call(
        kern,
        grid=(G // TG, FF // tf),
        in_specs=[
            pl.BlockSpec((TG, R, D), lambda i, f: (i, 0, 0), pipeline_mode=once),
            vec(4), vec(3),
            pl.BlockSpec((None, 1, D), lambda i, f: (layer, 0, 0), pipeline_mode=once),
            vec(5),
            _weight_spec(w_up, layer, (D, tf), lambda i, f: (0, f)),
            _weight_spec(w_down, layer, (tf, D), lambda i, f: (f, 0)),
            pl.BlockSpec((1, D), lambda i, f: (0, 0), pipeline_mode=once),
        ],
        out_specs=pl.BlockSpec((TG, R, D), lambda i, f: (i, 0, 0), pipeline_mode=once),
        out_shape=jax.ShapeDtypeStruct((G, R, D), F32),
        scratch_shapes=[pltpu.VMEM((TM, D), BF16)],
        compiler_params=_params(("parallel", "arbitrary"), vmem),
        name="ffn",
    )(x.reshape(G, R, D), mod, mod, g.reshape(-1, 1, D), mod, w_up, w_down,
      final_g.reshape(1, D)).reshape(B, T, D)


def _trunk(x, mod, row0, C0, n0, m0, S0, per_layer_state, wts, late=None):
    depth = mod.shape[0]
    made = []
    Cs, ns, ms, Ss = [], [], [], []
    for l in range(depth):
        sl = l if per_layer_state else 0
        z, zg = _in_proj(x, mod, row0, wts["norm1_g"], wts["w_main"], wts["w_gate"], l)
        hm, C, n, m = _mlstm(z, zg, wts["b_gate"][l], wts["mlstm_norm_g"][l], C0, n0, m0, sl)
        oh, S, cast = _hgrn(z, wts["lower_bounds"], l, wts["hgrn_norm_g"][l], S0, sl, col_block=3,
                            cast=wts["late_f32"] if late is None else ())
        w_out, w_up, w_down = cast if late is None else late[l]
        made.append((w_out, w_up, w_down))
        x = _out_proj(hm, oh, w_out, l, x, mod, row0)
        x = _ffn(x, mod, row0, wts["norm2_g"], w_up, w_down, l, wts["final_g"], final=(l == depth - 1))
        Cs.append(C); ns.append(n); ms.append(m); Ss.append(S)
    return x, jnp.stack(Cs), jnp.stack(ns), jnp.stack(ms), jnp.stack(Ss), made


def kernel(x_prompt, x_sample, state_mlstm_C, state_mlstm_n, state_mlstm_m, state_hgrn_S, c_prompt, c_sample, w_mod, b_mod, norm1_g, w_in, b_gate, lower_bounds, mlstm_norm_g, hgrn_norm_g, w_out, norm2_g, w_up, w_down, final_g):
    depth = w_mod.shape[0]
    Bp, Bs = x_prompt.shape[0], x_sample.shape[0]
    _, _, heads, dk, dv = state_mlstm_C.shape
    _, _, hh, hdk, hdv = state_hgrn_S.shape
    gate0 = 2 * heads * dk + 2 * heads * dv
    ngate = 2 * heads

    w_main, w_gate = _prep_w_in(w_in, gate0, ngate)
    wts = dict(norm1_g=norm1_g, w_main=w_main, w_gate=w_gate, b_gate=b_gate, lower_bounds=lower_bounds,
               mlstm_norm_g=mlstm_norm_g, hgrn_norm_g=hgrn_norm_g, norm2_g=norm2_g, final_g=final_g,
               late_f32=(w_out, w_up, w_down))
    late = None
    if not _hgrn_can_cast(*x_prompt.shape[:2], wts["late_f32"]):
        late = [(w_out.astype(BF16), w_up.astype(BF16), w_down.astype(BF16))] * depth

    pad_rows = lambda c: jnp.pad(c, ((0, -c.shape[0] % BF16_ROWS), (0, 0)))
    c_all = jnp.concatenate([pad_rows(c_prompt), pad_rows(c_sample)], axis=0)
    row_s = Bp + (-Bp % BF16_ROWS)
    mod = _modulation(c_all, w_mod, b_mod)
    mod = mod.reshape(depth, c_all.shape[0], 1, mod.shape[-1])

    zC = jnp.zeros((1, Bp, heads, dk, dv), F32)
    zn = jnp.zeros((1, Bp, heads, dk), F32)
    zm = jnp.zeros((1, Bp, heads), F32)
    zS = jnp.zeros((1, Bp, hh, hdk, hdv), F32)
    y_p, pC, pn, pm, pS, late = _trunk(x_prompt, mod, 0, zC, zn, zm, zS, False, wts, late)
    y_s, sC, sn, sm, sS, _ = _trunk(x_sample, mod, row_s, state_mlstm_C, state_mlstm_n, state_mlstm_m,
                                    state_hgrn_S, True, wts, late)
    return (y_p, y_s, pC, pn, pm, pS, sC, sn, sm, sS)
```

```python
import functools

import jax
import jax.numpy as jnp
import numpy as np
from jax import lax
from jax.experimental import pallas as pl
from jax.experimental.pallas import tpu as pltpu

EPS = 1e-6
NEG_BIG = -1e30
LB_FLOOR = 1e-30
MLSTM_CHUNK = 256
MLSTM_STEP = 512
MLSTM_BATCH = 1
HGRN_BLOCK = 16
HGRN_CHUNK = 128
LANES = 128
SUBLANES = 8
BF16_ROWS = 16
MIB = 1024 * 1024
VMEM_CAP = 63 * MIB

F32 = jnp.float32
BF16 = jnp.bfloat16


def _params(semantics, vmem_bytes):
    return pltpu.CompilerParams(dimension_semantics=semantics,
                                vmem_limit_bytes=int(min(VMEM_CAP, vmem_bytes)))


def _sigmoid(x):
    return 1.0 / (1.0 + jnp.exp(-x))


def _silu(x):
    return x * _sigmoid(x)


def _log_sigmoid(x):
    return jnp.minimum(x, 0.0) - jnp.log1p(jnp.exp(-jnp.abs(x)))


def _lower_tri(n, dtype):
    r = lax.broadcasted_iota(jnp.int32, (n, n), 0)
    c = lax.broadcasted_iota(jnp.int32, (n, n), 1)
    return jnp.where(r >= c, 1.0, 0.0).astype(dtype)


def _cumsum_rows(x, tri):
    hi = x.astype(BF16)
    r1 = x - hi.astype(F32)
    mid = r1.astype(BF16)
    lo = (r1 - mid.astype(F32)).astype(BF16)
    dot = functools.partial(jnp.dot, preferred_element_type=F32)
    return (dot(tri, hi) + dot(tri, mid)) + dot(tri, lo)


def _col_tile(n, pref):
    return max(t for t in range(LANES, min(pref, n) + 1, LANES) if n % t == 0)


def _row_groups(B, T, tm_pref):
    if T >= tm_pref:
        assert T % tm_pref == 0
        R, TG = tm_pref, 1
    else:
        R = T
        TG = max(1, min(B, tm_pref // T))
        assert B % TG == 0
    return (B * T) // R, R, TG, T // R


def _for_row_chunks(TG, R, body, step=4 * BF16_ROWS):
    step = step if R % step == 0 else R
    per = R // step

    def piece(c, carry):
        t = c // per
        r = pl.multiple_of((c % per) * step, step)
        body(pl.ds(t, 1), pl.ds(r, step), pl.ds(pl.multiple_of(t * R + r, step), step))
        return carry

    lax.fori_loop(0, TG * per, piece, 0)


def _mod_spec(mod, layer, row0, part, TG, reps, width, once=True):
    D = mod.shape[-1] // 6
    per = D // width
    mode = dict(pipeline_mode=pl.Buffered(1)) if once else {}
    col = (lambda j: part) if per == 1 else (lambda j: part * per + j)
    if reps > 1:
        assert TG == 1
        return pl.BlockSpec((None, 1, 1, width), lambda i, j: (layer, row0 + i // reps, 0, col(j)), **mode)
    assert row0 % TG == 0
    return pl.BlockSpec((None, TG, 1, width), lambda i, j: (layer, row0 // TG + i, 0, col(j)), **mode)


def _weight_spec(w, layer, block, index):
    if w.ndim == 2:
        return pl.BlockSpec(block, index)
    return pl.BlockSpec((None,) + block, lambda i, j: (layer,) + index(i, j))


def _mod_norm(x, g, sc, sh):
    ms = jnp.mean(x * x, axis=-1, keepdims=True)
    return (x * lax.rsqrt(ms + EPS) * g) * (1.0 + sc) + sh


def _mod_kernel(c_ref, w_ref, b_ref, o_ref):
    c = c_ref[...]
    a = _silu(c).astype(BF16)
    w = w_ref[0].astype(BF16)
    o_ref[0] = jnp.dot(a, w, preferred_element_type=F32) + b_ref[0]


def _modulation(c_all, w_mod, b_mod, *, tn=512):
    depth, D, N = w_mod.shape
    MP = c_all.shape[0]
    tn = _col_tile(N, tn)
    vmem = 2 * (D * tn * 4 + MP * tn * 4 + tn * 4) + 2 * MP * D * 4 + D * tn * 2 + 4 * MIB
    return pl.pallas_call(
        _mod_kernel,
        grid=(depth, N // tn),
        in_specs=[
            pl.BlockSpec((MP, D), lambda l, j: (0, 0)),
            pl.BlockSpec((1, D, tn), lambda l, j: (l, 0, j)),
            pl.BlockSpec((1, 1, tn), lambda l, j: (l, 0, j)),
        ],
        out_specs=pl.BlockSpec((1, MP, tn), lambda l, j: (l, 0, j)),
        out_shape=jax.ShapeDtypeStruct((depth, MP, N), F32),
        compiler_params=_params(("parallel", "parallel"), vmem),
        name="modulation",
    )(c_all, w_mod, b_mod.reshape(depth, 1, N))


def _win_kernel(w_ref, o_ref):
    o_ref[...] = w_ref[0].astype(BF16)


def _prep_w_in(w_in, gate0, ngate, *, rows=512):
    depth, D, NC = w_in.shape
    NZ = NC - ngate
    wT = jnp.swapaxes(w_in, 1, 2)
    gate = jnp.pad(wT[:, gate0:gate0 + ngate, :], ((0, 0), (0, LANES - ngate), (0, 0))).astype(BF16)
    if (gate0 + ngate) % SUBLANES:
        return jnp.concatenate([wT[:, :gate0], wT[:, gate0 + ngate:]], axis=1).astype(BF16), gate
    rows = max(r for r in range(BF16_ROWS, rows + 1, BF16_ROWS) if gate0 % r == 0 and NZ % r == 0)
    src_row = lambda c: pl.multiple_of(jnp.where(c * rows < gate0, c * rows, c * rows + ngate), SUBLANES)
    main = pl.pallas_call(
        _win_kernel,
        grid=(depth, NZ // rows),
        in_specs=[pl.BlockSpec((pl.Element(1), pl.Element(rows), pl.Element(D)), lambda l, c: (l, src_row(c), 0))],
        out_specs=pl.BlockSpec((None, rows, D), lambda l, c: (l, c, 0)),
        out_shape=jax.ShapeDtypeStruct((depth, NZ, D), BF16),
        compiler_params=_params(("parallel", "parallel"), 2 * rows * D * 6 + rows * D * 4 + 4 * MIB),
        name="w_in_prep",
    )(wT)
    return main, gate


def _in_kernel(x_ref, sc_ref, sh_ref, g_ref, w_ref, wg_ref, z_ref, zg_ref, h_scr):
    nt = (((1,), (1,)), ((), ()))

    @pl.when(pl.program_id(1) == 0)
    def _():
        def piece(gs, rs, fs):
            h = _mod_norm(x_ref[gs, rs, :], g_ref[...], sc_ref[gs], sh_ref[gs])
            hb = h.reshape(-1, h.shape[-1]).astype(BF16)
            h_scr[fs, :] = hb
            zg_ref[fs, :] = lax.dot_general(hb, wg_ref[...], nt, preferred_element_type=F32)

        _for_row_chunks(*x_ref.shape[:2], piece)

    z_ref[...] = lax.dot_general(h_scr[...], w_ref[...], nt, preferred_element_type=F32)


def _in_proj(x, mod, row0, g, w_main, w_gate, layer, *, tm=512, tn=1024):
    B, T, D = x.shape
    NZ = w_main.shape[1]
    G, R, TG, reps = _row_groups(B, T, tm)
    TM, M = TG * R, B * T
    tn = _col_tile(NZ, tn)
    vmem = 2 * TM * D * 4 + D * LANES * 2 + (2 * TG + 1) * SUBLANES * D * 4 \
        + 2 * (D * tn * 2 + TM * tn * 4 + TM * LANES * 4) + TM * D * 2 + 8 * MIB
    once = pl.Buffered(1)
    z, zg = pl.pallas_call(
        _in_kernel,
        grid=(G // TG, NZ // tn),
        in_specs=[
            pl.BlockSpec((TG, R, D), lambda i, j: (i, 0, 0)),
            _mod_spec(mod, layer, row0, 1, TG, reps, D),
            _mod_spec(mod, layer, row0, 0, TG, reps, D),
            pl.BlockSpec((None, 1, D), lambda i, j: (layer, 0, 0), pipeline_mode=once),
            pl.BlockSpec((None, tn, D), lambda i, j: (layer, j, 0)),
            pl.BlockSpec((None, LANES, D), lambda i, j: (layer, 0, 0), pipeline_mode=once),
        ],
        out_specs=[
            pl.BlockSpec((TM, tn), lambda i, j: (i, j)),
            pl.BlockSpec((TM, LANES), lambda i, j: (i, 0)),
        ],
        out_shape=[jax.ShapeDtypeStruct((M, NZ), F32), jax.ShapeDtypeStruct((M, LANES), F32)],
        scratch_shapes=[pltpu.VMEM((TM, D), BF16)],
        compiler_params=_params(("parallel", "arbitrary"), vmem),
        name="in_proj",
    )(x.reshape(G, R, D), mod, mod, g.reshape(-1, 1, D), w_main, w_gate)
    return z.reshape(B, T, NZ), zg.reshape(B, T, LANES)


def _mlstm_kernel(q_ref, k_ref, v_ref, o_ref, zg_ref, bg_ref, mg_ref, C0_ref, n0_ref, m0_ref,
                  hm_ref, C_ref, n_ref, m_ref, *, heads, dk, dv, chunk, nsub):
    L = chunk

    @pl.when(pl.program_id(1) == 0)
    def _():
        C_ref[...] = C0_ref[0]
        n_ref[...] = n0_ref[0]
        m_ref[...] = m0_ref[0]

    tri = _lower_tri(L, BF16)
    causal = (lax.broadcasted_iota(jnp.int32, (L, L), 0) >= lax.broadcasted_iota(jnp.int32, (L, L), 1))
    for s in range(nsub):
        for bb in range(q_ref.shape[0]):
            _mlstm_chunk(bb, slice(s * L, (s + 1) * L), tri, causal, q_ref, k_ref, v_ref, o_ref, zg_ref, bg_ref,
                         mg_ref, hm_ref, C_ref, n_ref, m_ref, heads=heads, dk=dk, dv=dv, L=L)


def _mlstm_chunk(bb, r, tri, causal, q_ref, k_ref, v_ref, o_ref, zg_ref, bg_ref, mg_ref,
                 hm_ref, C_ref, n_ref, m_ref, *, heads, dk, dv, L):
    gates = zg_ref[bb, r, :] + bg_ref[...]
    bsum = _cumsum_rows(_log_sigmoid(gates), tri)
    lane = lax.broadcasted_iota(jnp.int32, gates.shape, 1)
    rows = jnp.where(lane < heads, gates, bsum).T

    for h in range(heads):
        i_col = gates[:, h:h + 1]
        b_col = bsum[:, heads + h:heads + h + 1]
        i_row = rows[h:h + 1, :]
        b_row = rows[heads + h:heads + h + 1, :]
        m_prev = m_ref[bb, h:h + 1, 0:1]
        n_prev = n_ref[bb, h:h + 1, :]
        C_prev = C_ref[bb, h]

        q = q_ref[bb, r, h * dk:(h + 1) * dk] * (dk ** -0.5)
        k = k_ref[bb, r, h * dk:(h + 1) * dk]
        vb = v_ref[bb, r, h * dv:(h + 1) * dv].astype(BF16)
        qb = q.astype(BF16)

        D = jnp.where(causal, (b_col - b_row) + i_row, NEG_BIG)
        inter = b_col + m_prev
        mt = jnp.maximum(inter, jnp.max(D, axis=-1, keepdims=True))
        a_inter = jnp.exp(inter - mt)
        S = lax.dot_general(qb, k.astype(BF16), (((1,), (1,)), ((), ())),
                            preferred_element_type=F32) * jnp.exp(D - mt)
        num = a_inter * jnp.dot(qb, C_prev.astype(BF16), preferred_element_type=F32) \
            + jnp.dot(S.astype(BF16), vb, preferred_element_type=F32)
        den = a_inter * jnp.sum(q * n_prev, axis=-1, keepdims=True) + jnp.sum(S, axis=-1, keepdims=True)
        hh = num / jnp.maximum(jnp.abs(den), jnp.exp(-mt))

        b_last = b_col[L - 1:L, :]
        wlast = (b_last - b_col) + i_col
        m_new = jnp.maximum(b_last + m_prev, jnp.max(wlast, axis=0, keepdims=True))
        a_c = jnp.exp((b_last + m_prev) - m_new)
        kw = k * jnp.exp(wlast - m_new)
        C_ref[bb, h] = a_c * C_prev + lax.dot_general(kw.astype(BF16), vb, (((0,), (0,)), ((), ())),
                                                     preferred_element_type=F32)
        n_ref[bb, h:h + 1, :] = a_c * n_prev + jnp.sum(kw, axis=0, keepdims=True)
        m_ref[bb, h:h + 1, :] = jnp.broadcast_to(m_new, (1, LANES))

        ms = jnp.mean(hh * hh, axis=-1, keepdims=True)
        y = (hh * lax.rsqrt(ms + EPS)) * mg_ref[:, h * dv:(h + 1) * dv]
        y = y * _sigmoid(o_ref[bb, r, h * dv:(h + 1) * dv])
        hm_ref[bb, r, h * dv:(h + 1) * dv] = y.astype(BF16)


def _mlstm(z, zg, b_gate, norm_g, C0, n0, m0, sl):
    B, T, _ = z.shape
    _, _, heads, dk, dv = C0.shape
    QW, VW = heads * dk, heads * dv
    assert VW == 2 * QW and 2 * heads <= LANES
    L = MLSTM_CHUNK if T % MLSTM_CHUNK == 0 else T
    Ls = MLSTM_STEP if (T % MLSTM_STEP == 0 and MLSTM_STEP % L == 0) else L
    BB = MLSTM_BATCH if B % MLSTM_BATCH == 0 else 1
    bias = jnp.zeros((1, LANES), F32).at[0, :2 * heads].set(b_gate)
    m0b = jnp.broadcast_to(m0[..., None], m0.shape + (LANES,))
    vmem = BB * (2 * (Ls * (2 * QW + 2 * VW + LANES) * 4 + Ls * VW * 2) + 4 * heads * dk * dv * 4) + 12 * MIB
    kern = functools.partial(_mlstm_kernel, heads=heads, dk=dk, dv=dv, chunk=L, nsub=Ls // L)
    hm, C, n, m = pl.pallas_call(
        kern,
        grid=(B // BB, T // Ls),
        in_specs=[
            pl.BlockSpec((BB, Ls, QW), lambda b, c: (b, c, 0)),
            pl.BlockSpec((BB, Ls, QW), lambda b, c: (b, c, 1)),
            pl.BlockSpec((BB, Ls, VW), lambda b, c: (b, c, 1)),
            pl.BlockSpec((BB, Ls, VW), lambda b, c: (b, c, 2)),
            pl.BlockSpec((BB, Ls, LANES), lambda b, c: (b, c, 0)),
            pl.BlockSpec((1, LANES), lambda b, c: (0, 0)),
            pl.BlockSpec((1, VW), lambda b, c: (0, 0)),
            pl.BlockSpec((1, BB, heads, dk, dv), lambda b, c: (sl, b, 0, 0, 0)),
            pl.BlockSpec((1, BB, heads, dk), lambda b, c: (sl, b, 0, 0)),
            pl.BlockSpec((1, BB, heads, LANES), lambda b, c: (sl, b, 0, 0)),
        ],
        out_specs=[
            pl.BlockSpec((BB, Ls, VW), lambda b, c: (b, c, 0)),
            pl.BlockSpec((BB, heads, dk, dv), lambda b, c: (b, 0, 0, 0)),
            pl.BlockSpec((BB, heads, dk), lambda b, c: (b, 0, 0)),
            pl.BlockSpec((BB, heads, LANES), lambda b, c: (b, 0, 0)),
        ],
        out_shape=[
            jax.ShapeDtypeStruct((B, T, VW), BF16),
            jax.ShapeDtypeStruct((B, heads, dk, dv), F32),
            jax.ShapeDtypeStruct((B, heads, dk), F32),
            jax.ShapeDtypeStruct((B, heads, LANES), F32),
        ],
        compiler_params=_params(("parallel", "arbitrary"), vmem),
        name="mlstm",
    )(z, z, z, z, zg, bias, norm_g.reshape(1, VW), C0, n0, m0b)
    return hm, C, n, m[:, :, 0]


def _hgrn_tables(nb):
    u = np.arange(nb)[:, None]
    j = np.arange(nb)[None, :]
    mats, masks = [j <= u], []
    w = nb // 2
    while w >= 1:
        r = (u // (2 * w)) * (2 * w) + w - 1
        upper = (u % (2 * w)) >= w
        mats.append(np.where(upper, (j > r) & (j <= u), (j > u) & (j <= r)))
        masks.append(((u // (2 * w)) == (j // (2 * w))) & upper & ((j % (2 * w)) < w))
        w //= 2
    mats.append(j > u)
    masks.append(u == j)
    masks = np.stack(masks).astype(np.float32)
    return np.concatenate(mats, 0).astype(np.float32), np.concatenate([masks, masks], axis=2)


def _hgrn_kernel(*refs, heads, dk, dv, layer, chunk, ncast):
    q_ref, f_ref, i_ref, g_ref, lb_ref, ng_ref, S0_ref, mst_ref, lvl_ref = refs[:9]
    oh_ref, S_ref = refs[9 + ncast:11 + ncast]
    st_scr = refs[-1]
    nb = HGRN_BLOCK
    nlev = lvl_ref.shape[0]
    c = pl.program_id(1)

    for src, dst in zip(refs[9:9 + ncast], refs[11 + ncast:11 + 2 * ncast]):
        dst[...] = src[...].astype(BF16)

    @pl.when(c == 0)
    def _():
        for h in range(heads):
            st_scr[h] = S0_ref[0, 0, h].T

    lbs = lb_ref[...]
    e = jnp.exp(lbs - jnp.max(lbs, axis=0, keepdims=True))
    sm = e / jnp.sum(e, axis=0, keepdims=True)
    cum = sm[0:1, :]
    for j in range(1, layer + 1):
        cum = cum + sm[j:j + 1, :]
    lb = cum - sm[0:1, :]
    lb_floor = jnp.maximum(lb, LB_FLOOR)
    one_m = 1.0 - lb

    mst = mst_ref[...]
    masks = [lvl_ref[i] > 0.5 for i in range(nlev)]
    nt = (((1,), (1,)), ((), ()))
    first = lax.broadcasted_iota(jnp.int32, (nb, 2 * dk), 1) < dk

    def pair_diag(a):
        zero = jnp.zeros_like(a)
        return jnp.concatenate([jnp.where(first, a, zero), jnp.where(first, zero, a)], axis=0)

    def block(j, carry):
        rows = pl.ds(j * nb, nb)
        fx = f_ref[0, rows, :]
        en = jnp.exp(-jnp.abs(fx))
        rc = 1.0 / (1.0 + en)
        er = en * rc
        pos = fx >= 0.0
        g = jnp.log(lb_floor + one_m * jnp.where(pos, rc, er))
        kk = one_m * jnp.where(pos, er, rc)
        qs = _silu(q_ref[0, rows, :])
        vb = i_ref[0, rows, :].astype(BF16)
        gate = _silu(g_ref[0, rows, :])

        ghi = g.astype(BF16)
        gmid = (g - ghi.astype(F32)).astype(BF16)
        E = jnp.exp(jnp.dot(mst, ghi, preferred_element_type=F32)
                    + jnp.dot(mst, gmid, preferred_element_type=F32))
        EG = E[:nb]
        qG = (qs * EG).astype(BF16)
        kL = (kk * E[nlev * nb:]).astype(BF16)
        Ql = [(qs * E[(i + 1) * nb:(i + 2) * nb]).astype(BF16) for i in range(nlev - 1)] + [qs.astype(BF16)]
        Kl = [(kk * E[(i + 1) * nb:(i + 2) * nb]).astype(BF16) for i in range(nlev - 1)] + [kk.astype(BF16)]
        decay = EG[nb - 1:nb]

        pairs = [slice(2 * p * dk, (2 * p + 2) * dk) for p in range(heads // 2)]
        A2 = []
        for sl2 in pairs:
            a = jnp.zeros((nb, 2 * nb), F32)
            for i in range(nlev):
                a = jnp.where(masks[i], lax.dot_general(Ql[i][:, sl2], pair_diag(Kl[i][:, sl2]), nt,
                                                        preferred_element_type=F32), a)
            A2.append(a.astype(BF16))
        o_intra = jnp.concatenate(
            [jnp.dot(a, pair_diag(vb[:, sl2]), preferred_element_type=F32) for a, sl2 in zip(A2, pairs)], axis=1)

        for h in range(heads):
            sl = slice(h * dk, (h + 1) * dk)
            st = st_scr[h]
            o = lax.dot_general(qG[:, sl], st.astype(BF16), nt, preferred_element_type=F32)
            o = o + o_intra[:, sl]
            upd = lax.dot_general(vb[:, sl], kL[:, sl], (((0,), (0,)), ((), ())),
                                  preferred_element_type=F32)
            st_scr[h] = st * decay[:, sl] + upd

            ms = jnp.mean(o * o, axis=-1, keepdims=True)
            y = (o * lax.rsqrt(ms + EPS)) * ng_ref[:, sl] * gate[:, sl]
            oh_ref[0, rows, sl] = y.astype(BF16)
        return carry

    for j in range(chunk // nb):
        block(j, 0)

    @pl.when(c == pl.num_programs(1) - 1)
    def _():
        for h in range(heads):
            S_ref[0, h] = st_scr[h].T


def _hgrn_steps(B, T):
    Lc = HGRN_CHUNK if T % HGRN_CHUNK == 0 else T
    return Lc, B * (T // Lc)


def _hgrn_can_cast(B, T, weights):
    _, steps = _hgrn_steps(B, T)
    return all(w.shape[1] % (steps * BF16_ROWS) == 0 for w in weights)


def _hgrn(z, lower_bounds, layer, norm_g, S0, sl, col_block, cast=()):
    B, T, _ = z.shape
    _, _, heads, dk, dv = S0.shape
    HW = heads * dk
    assert dk == dv
    Lc, steps = _hgrn_steps(B, T)
    NC = T // Lc
    assert Lc % HGRN_BLOCK == 0
    depth = lower_bounds.shape[0]
    mst, lvl = _hgrn_tables(HGRN_BLOCK)
    mst = jnp.asarray(mst, BF16)
    lvl = jnp.asarray(lvl, F32)
    slabs = [(w.shape[1] // steps, w.shape[2]) for w in cast]
    vmem = 2 * (4 * Lc * HW * 4 + Lc * HW * 2) + 5 * heads * dk * dv * 4 \
        + 6 * mst.shape[0] * HW * 4 + sum(2 * rb * cols * 6 for rb, cols in slabs) + 16 * MIB
    kern = functools.partial(_hgrn_kernel, heads=heads, dk=dk, dv=dv, layer=layer, chunk=Lc, ncast=len(cast))
    zspec = lambda idx: pl.BlockSpec((1, Lc, HW), lambda b, c: (b, c, idx))
    outs = pl.pallas_call(
        kern,
        grid=(B, NC),
        in_specs=[
            zspec(col_block), zspec(col_block + 1), zspec(col_block + 2), zspec(col_block + 3),
            pl.BlockSpec((depth, HW), lambda b, c: (0, 0)),
            pl.BlockSpec((1, HW), lambda b, c: (0, 0)),
            pl.BlockSpec((1, 1, heads, dk, dv), lambda b, c: (sl, b, 0, 0, 0)),
            pl.BlockSpec(mst.shape, lambda b, c: (0, 0)),
            pl.BlockSpec(lvl.shape, lambda b, c: (0, 0, 0)),
        ] + [pl.BlockSpec((None, rb, cols), lambda b, c: (layer, b * NC + c, 0)) for rb, cols in slabs],
        out_specs=[
            pl.BlockSpec((1, Lc, HW), lambda b, c: (b, c, 0)),
            pl.BlockSpec((1, heads, dk, dv), lambda b, c: (b, 0, 0, 0)),
        ] + [pl.BlockSpec((rb, cols), lambda b, c: (b * NC + c, 0)) for rb, cols in slabs],
        out_shape=[
            jax.ShapeDtypeStruct((B, T, HW), BF16),
            jax.ShapeDtypeStruct((B, heads, dk, dv), F32),
        ] + [jax.ShapeDtypeStruct(w.shape[1:], BF16) for w in cast],
        scratch_shapes=[pltpu.VMEM((heads, dv, dk), F32)],
        compiler_params=_params(("parallel", "arbitrary"), vmem),
        name="hgrn",
    )(z, z, z, z, lower_bounds, norm_g.reshape(1, HW), S0, mst, lvl, *cast)
    return outs[0], outs[1], list(outs[2:])


def _out_kernel(hm_ref, oh_ref, wa_ref, wb_ref, x_ref, ga_ref, xo_ref):
    acc = jnp.dot(hm_ref[...], wa_ref[...], preferred_element_type=F32)
    acc = acc + jnp.dot(oh_ref[...], wb_ref[...], preferred_element_type=F32)
    xo_ref[...] = x_ref[...] + ga_ref[...] * acc.reshape(xo_ref.shape)


def _out_proj(hm, oh, w_out, layer, x, mod, row0, *, tm=1024, tn=1024):
    B, T, D = x.shape
    MW, HW = hm.shape[-1], oh.shape[-1]
    assert MW == HW and w_out.shape[-2] == MW + HW
    G, R, TG, reps = _row_groups(B, T, tm)
    TM, M = TG * R, B * T
    tn = _col_tile(D, tn)
    vmem = 2 * (2 * TM * MW * 2 + 2 * MW * tn * 2 + 2 * TM * tn * 4 + tn * 4) + 2 * TM * tn * 4 + 4 * MIB
    return pl.pallas_call(
        _out_kernel,
        grid=(G // TG, D // tn),
        in_specs=[
            pl.BlockSpec((TM, MW), lambda i, j: (i, 0)),
            pl.BlockSpec((TM, HW), lambda i, j: (i, 0)),
            _weight_spec(w_out, layer, (MW, tn), lambda i, j: (0, j)),
            _weight_spec(w_out, layer, (HW, tn), lambda i, j: (1, j)),
            pl.BlockSpec((TG, R, tn), lambda i, j: (i, 0, j)),
            _mod_spec(mod, layer, row0, 2, TG, reps, tn, once=False),
        ],
        out_specs=pl.BlockSpec((TG, R, tn), lambda i, j: (i, 0, j)),
        out_shape=jax.ShapeDtypeStruct((G, R, D), F32),
        compiler_params=_params(("parallel", "arbitrary"), vmem),
        name="out_proj",
    )(hm.reshape(M, MW), oh.reshape(M, HW), w_out, w_out, x.reshape(G, R, D), mod).reshape(B, T, D)


def _ffn_kernel(x_ref, sc_ref, sh_ref, g_ref, ga_ref, wu_ref, wd_ref, fg_ref, o_ref, h_scr,
                *, final, n_split, f_split):
    f = pl.program_id(1)
    D = o_ref.shape[-1]
    dn = D // n_split
    ts = wu_ref.shape[-1] // f_split

    @pl.when(f == 0)
    def _():
        def piece(gs, rs, fs):
            h = _mod_norm(x_ref[gs, rs, :], g_ref[...], sc_ref[gs], sh_ref[gs])
            h_scr[fs, :] = h.reshape(-1, D).astype(BF16)
            o_ref[gs, rs, :] = jnp.zeros_like(h)

        _for_row_chunks(*x_ref.shape[:2], piece)

    blk = o_ref.shape[:-1] + (dn,)
    for k in range(f_split):
        u = jnp.dot(h_scr[...], wu_ref[:, k * ts:(k + 1) * ts], preferred_element_type=F32)
        a = jnp.square(jnp.maximum(u, 0.0)).astype(BF16)
        for n in range(n_split):
            sl = slice(n * dn, (n + 1) * dn)
            o_ref[:, :, sl] += jnp.dot(a, wd_ref[k * ts:(k + 1) * ts, sl], preferred_element_type=F32).reshape(blk)

    @pl.when(f == pl.num_programs(1) - 1)
    def _():
        def piece(gs, rs, fs):
            xo = x_ref[gs, rs, :] + ga_ref[gs] * o_ref[gs, rs, :]
            if final:
                ms = jnp.mean(xo * xo, axis=-1, keepdims=True)
                xo = (xo * lax.rsqrt(ms + EPS)) * fg_ref[...]
            o_ref[gs, rs, :] = xo

        _for_row_chunks(*x_ref.shape[:2], piece)


def _ffn(x, mod, row0, g, w_up, w_down, layer, final_g, *, final, tm=512, tf=1024):
    B, T, D = x.shape
    FF = w_up.shape[-1]
    G, R, TG, reps = _row_groups(B, T, tm)
    TM = TG * R
    tf = _col_tile(FF, tf)
    n_split = 4 if D % (4 * LANES) == 0 else 1
    f_split = 2 if tf % (2 * LANES) == 0 else 1
    vmem = 2 * TM * D * 4 + 2 * 2 * D * tf * 2 + (3 * TG + 4) * SUBLANES * D * 4 + TM * D * 2 + TM * tf * 6 \
        + 2 * TM * (D // n_split) * 4 + 4 * MIB
    kern = functools.partial(_ffn_kernel, final=final, n_split=n_split, f_split=f_split)
    once = pl.Buffered(1)
    vec = lambda part: _mod_spec(mod, layer, row0, part, TG, reps, D)
    return pl.pallas_call(
        kern,
        grid=(G // TG, FF // tf),
        in_specs=[
            pl.BlockSpec((TG, R, D), lambda i, f: (i, 0, 0), pipeline_mode=once),
            vec(4), vec(3),
            pl.BlockSpec((None, 1, D), lambda i, f: (layer, 0, 0), pipeline_mode=once),
            vec(5),
            _weight_spec(w_up, layer, (D, tf), lambda i, f: (0, f)),
            _weight_spec(w_down, layer, (tf, D), lambda i, f: (f, 0)),
            pl.BlockSpec((1, D), lambda i, f: (0, 0), pipeline_mode=once),
        ],
        out_specs=pl.BlockSpec((TG, R, D), lambda i, f: (i, 0, 0), pipeline_mode=once),
        out_shape=jax.ShapeDtypeStruct((G, R, D), F32),
        scratch_shapes=[pltpu.VMEM((TM, D), BF16)],
        compiler_params=_params(("parallel", "arbitrary"), vmem),
        name="ffn",
    )(x.reshape(G, R, D), mod, mod, g.reshape(-1, 1, D), mod, w_up, w_down,
      final_g.reshape(1, D)).reshape(B, T, D)


def _trunk(x, mod, row0, C0, n0, m0, S0, per_layer_state, wts, late=None):
    depth = mod.shape[0]
    made = []
    Cs, ns, ms, Ss = [], [], [], []
    for l in range(depth):
        sl = l if per_layer_state else 0
        z, zg = _in_proj(x, mod, row0, wts["norm1_g"], wts["w_main"], wts["w_gate"], l)
        hm, C, n, m = _mlstm(z, zg, wts["b_gate"][l], wts["mlstm_norm_g"][l], C0, n0, m0, sl)
        oh, S, cast = _hgrn(z, wts["lower_bounds"], l, wts["hgrn_norm_g"][l], S0, sl, col_block=3,
                            cast=wts["late_f32"] if late is None else ())
        w_out, w_up, w_down = cast if late is None else late[l]
        made.append((w_out, w_up, w_down))
        x = _out_proj(hm, oh, w_out, l, x, mod, row0)
        x = _ffn(x, mod, row0, wts["norm2_g"], w_up, w_down, l, wts["final_g"], final=(l == depth - 1))
        Cs.append(C); ns.append(n); ms.append(m); Ss.append(S)
    return x, jnp.stack(Cs), jnp.stack(ns), jnp.stack(ms), jnp.stack(Ss), made


def kernel(x_prompt, x_sample, state_mlstm_C, state_mlstm_n, state_mlstm_m, state_hgrn_S, c_prompt, c_sample, w_mod, b_mod, norm1_g, w_in, b_gate, lower_bounds, mlstm_norm_g, hgrn_norm_g, w_out, norm2_g, w_up, w_down, final_g):
    depth = w_mod.shape[0]
    Bp, Bs = x_prompt.shape[0], x_sample.shape[0]
    _, _, heads, dk, dv = state_mlstm_C.shape
    _, _, hh, hdk, hdv = state_hgrn_S.shape
    gate0 = 2 * heads * dk + 2 * heads * dv
    ngate = 2 * heads

    w_main, w_gate = _prep_w_in(w_in, gate0, ngate)
    wts = dict(norm1_g=norm1_g, w_main=w_main, w_gate=w_gate, b_gate=b_gate, lower_bounds=lower_bounds,
               mlstm_norm_g=mlstm_norm_g, hgrn_norm_g=hgrn_norm_g, norm2_g=norm2_g, final_g=final_g,
               late_f32=(w_out, w_up, w_down))
    late = None
    if not _hgrn_can_cast(*x_prompt.shape[:2], wts["late_f32"]):
        late = [(w_out.astype(BF16), w_up.astype(BF16), w_down.astype(BF16))] * depth

    pad_rows = lambda c: jnp.pad(c, ((0, -c.shape[0] % BF16_ROWS), (0, 0)))
    c_all = jnp.concatenate([pad_rows(c_prompt), pad_rows(c_sample)], axis=0)
    row_s = Bp + (-Bp % BF16_ROWS)
    mod = _modulation(c_all, w_mod, b_mod)
    mod = mod.reshape(depth, c_all.shape[0], 1, mod.shape[-1])

    zC = jnp.zeros((1, Bp, heads, dk, dv), F32)
    zn = jnp.zeros((1, Bp, heads, dk), F32)
    zm = jnp.zeros((1, Bp, heads), F32)
    zS = jnp.zeros((1, Bp, hh, hdk, hdv), F32)
    y_p, pC, pn, pm, pS, late = _trunk(x_prompt, mod, 0, zC, zn, zm, zS, False, wts, late)
    y_s, sC, sn, sm, sS, _ = _trunk(x_sample, mod, row_s, state_mlstm_C, state_mlstm_n, state_mlstm_m,
                                    state_hgrn_S, True, wts, late)
    return (y_p, y_s, pC, pn, pm, pS, sC, sn, sm, sS)
```

```python
import functools

import jax
import jax.numpy as jnp
import numpy as np
from jax import lax
from jax.experimental import pallas as pl
from jax.experimental.pallas import tpu as pltpu

EPS = 1e-6
NEG_BIG = -1e30
LB_FLOOR = 1e-30
MLSTM_CHUNK = 256
MLSTM_STEP = 512
MLSTM_BATCH = 1
HGRN_BLOCK = 16
HGRN_CHUNK = 128
LANES = 128
SUBLANES = 8
BF16_ROWS = 16
MIB = 1024 * 1024
VMEM_CAP = 63 * MIB

F32 = jnp.float32
BF16 = jnp.bfloat16


def _params(semantics, vmem_bytes):
    return pltpu.CompilerParams(dimension_semantics=semantics,
                                vmem_limit_bytes=int(min(VMEM_CAP, vmem_bytes)))


def _sigmoid(x):
    return 1.0 / (1.0 + jnp.exp(-x))


def _silu(x):
    return x * _sigmoid(x)


def _log_sigmoid(x):
    return jnp.minimum(x, 0.0) - jnp.log1p(jnp.exp(-jnp.abs(x)))


def _lower_tri(n, dtype):
    r = lax.broadcasted_iota(jnp.int32, (n, n), 0)
    c = lax.broadcasted_iota(jnp.int32, (n, n), 1)
    return jnp.where(r >= c, 1.0, 0.0).astype(dtype)


def _cumsum_rows(x, tri):
    hi = x.astype(BF16)
    r1 = x - hi.astype(F32)
    mid = r1.astype(BF16)
    lo = (r1 - mid.astype(F32)).astype(BF16)
    dot = functools.partial(jnp.dot, preferred_element_type=F32)
    return (dot(tri, hi) + dot(tri, mid)) + dot(tri, lo)


def _col_tile(n, pref):
    return max(t for t in range(LANES, min(pref, n) + 1, LANES) if n % t == 0)


def _row_groups(B, T, tm_pref):
    if T >= tm_pref:
        assert T % tm_pref == 0
        R, TG = tm_pref, 1
    else:
        R = T
        TG = max(1, min(B, tm_pref // T))
        assert B % TG == 0
    return (B * T) // R, R, TG, T // R


def _for_row_chunks(TG, R, body, step=4 * BF16_ROWS):
    step = step if R % step == 0 else R
    per = R // step

    def piece(c, carry):
        t = c // per
        r = pl.multiple_of((c % per) * step, step)
        body(pl.ds(t, 1), pl.ds(r, step), pl.ds(pl.multiple_of(t * R + r, step), step))
        return carry

    lax.fori_loop(0, TG * per, piece, 0)


def _mod_spec(mod, layer, row0, part, TG, reps, width, once=True):
    D = mod.shape[-1] // 6
    per = D // width
    mode = dict(pipeline_mode=pl.Buffered(1)) if once else {}
    col = (lambda j: part) if per == 1 else (lambda j: part * per + j)
    if reps > 1:
        assert TG == 1
        return pl.BlockSpec((None, 1, 1, width), lambda i, j: (layer, row0 + i // reps, 0, col(j)), **mode)
    assert row0 % TG == 0
    return pl.BlockSpec((None, TG, 1, width), lambda i, j: (layer, row0 // TG + i, 0, col(j)), **mode)


def _weight_spec(w, layer, block, index):
    if w.ndim == 2:
        return pl.BlockSpec(block, index)
    return pl.BlockSpec((None,) + block, lambda i, j: (layer,) + index(i, j))


def _mod_norm(x, g, sc, sh):
    ms = jnp.mean(x * x, axis=-1, keepdims=True)
    return (x * lax.rsqrt(ms + EPS) * g) * (1.0 + sc) + sh


def _mod_kernel(c_ref, w_ref, b_ref, o_ref):
    c = c_ref[...]
    a = _silu(c).astype(BF16)
    w = w_ref[0].astype(BF16)
    o_ref[0] = jnp.dot(a, w, preferred_element_type=F32) + b_ref[0]


def _modulation(c_all, w_mod, b_mod, *, tn=512):
    depth, D, N = w_mod.shape
    MP = c_all.shape[0]
    tn = _col_tile(N, tn)
    vmem = 2 * (D * tn * 4 + MP * tn * 4 + tn * 4) + 2 * MP * D * 4 + D * tn * 2 + 4 * MIB
    return pl.pallas_call(
        _mod_kernel,
        grid=(depth, N // tn),
        in_specs=[
            pl.BlockSpec((MP, D), lambda l, j: (0, 0)),
            pl.BlockSpec((1, D, tn), lambda l, j: (l, 0, j)),
            pl.BlockSpec((1, 1, tn), lambda l, j: (l, 0, j)),
        ],
        out_specs=pl.BlockSpec((1, MP, tn), lambda l, j: (l, 0, j)),
        out_shape=jax.ShapeDtypeStruct((depth, MP, N), F32),
        compiler_params=_params(("parallel", "parallel"), vmem),
        name="modulation",
    )(c_all, w_mod, b_mod.reshape(depth, 1, N))


def _win_kernel(w_ref, o_ref):
    o_ref[...] = w_ref[0].astype(BF16)


def _prep_w_in(w_in, gate0, ngate, *, rows=512):
    depth, D, NC = w_in.shape
    NZ = NC - ngate
    wT = jnp.swapaxes(w_in, 1, 2)
    gate = jnp.pad(wT[:, gate0:gate0 + ngate, :], ((0, 0), (0, LANES - ngate), (0, 0))).astype(BF16)
    if (gate0 + ngate) % SUBLANES:
        return jnp.concatenate([wT[:, :gate0], wT[:, gate0 + ngate:]], axis=1).astype(BF16), gate
    rows = max(r for r in range(BF16_ROWS, rows + 1, BF16_ROWS) if gate0 % r == 0 and NZ % r == 0)
    src_row = lambda c: pl.multiple_of(jnp.where(c * rows < gate0, c * rows, c * rows + ngate), SUBLANES)
    main = pl.pallas_call(
        _win_kernel,
        grid=(depth, NZ // rows),
        in_specs=[pl.BlockSpec((pl.Element(1), pl.Element(rows), pl.Element(D)), lambda l, c: (l, src_row(c), 0))],
        out_specs=pl.BlockSpec((None, rows, D), lambda l, c: (l, c, 0)),
        out_shape=jax.ShapeDtypeStruct((depth, NZ, D), BF16),
        compiler_params=_params(("parallel", "parallel"), 2 * rows * D * 6 + rows * D * 4 + 4 * MIB),
        name="w_in_prep",
    )(wT)
    return main, gate


def _in_kernel(x_ref, sc_ref, sh_ref, g_ref, w_ref, wg_ref, z_ref, zg_ref, h_scr):
    nt = (((1,), (1,)), ((), ()))

    @pl.when(pl.program_id(1) == 0)
    def _():
        def piece(gs, rs, fs):
            h = _mod_norm(x_ref[gs, rs, :], g_ref[...], sc_ref[gs], sh_ref[gs])
            hb = h.reshape(-1, h.shape[-1]).astype(BF16)
            h_scr[fs, :] = hb
            zg_ref[fs, :] = lax.dot_general(hb, wg_ref[...], nt, preferred_element_type=F32)

        _for_row_chunks(*x_ref.shape[:2], piece)

    z_ref[...] = lax.dot_general(h_scr[...], w_ref[...], nt, preferred_element_type=F32)


def _in_proj(x, mod, row0, g, w_main, w_gate, layer, *, tm=512, tn=1024):
    B, T, D = x.shape
    NZ = w_main.shape[1]
    G, R, TG, reps = _row_groups(B, T, tm)
    TM, M = TG * R, B * T
    tn = _col_tile(NZ, tn)
    vmem = 2 * TM * D * 4 + D * LANES * 2 + (2 * TG + 1) * SUBLANES * D * 4 \
        + 2 * (D * tn * 2 + TM * tn * 4 + TM * LANES * 4) + TM * D * 2 + 8 * MIB
    once = pl.Buffered(1)
    z, zg = pl.pallas_call(
        _in_kernel,
        grid=(G // TG, NZ // tn),
        in_specs=[
            pl.BlockSpec((TG, R, D), lambda i, j: (i, 0, 0)),
            _mod_spec(mod, layer, row0, 1, TG, reps, D),
            _mod_spec(mod, layer, row0, 0, TG, reps, D),
            pl.BlockSpec((None, 1, D), lambda i, j: (layer, 0, 0), pipeline_mode=once),
            pl.BlockSpec((None, tn, D), lambda i, j: (layer, j, 0)),
            pl.BlockSpec((None, LANES, D), lambda i, j: (layer, 0, 0), pipeline_mode=once),
        ],
        out_specs=[
            pl.BlockSpec((TM, tn), lambda i, j: (i, j)),
            pl.BlockSpec((TM, LANES), lambda i, j: (i, 0)),
        ],
        out_shape=[jax.ShapeDtypeStruct((M, NZ), F32), jax.ShapeDtypeStruct((M, LANES), F32)],
        scratch_shapes=[pltpu.VMEM((TM, D), BF16)],
        compiler_params=_params(("parallel", "arbitrary"), vmem),
        name="in_proj",
    )(x.reshape(G, R, D), mod, mod, g.reshape(-1, 1, D), w_main, w_gate)
    return z.reshape(B, T, NZ), zg.reshape(B, T, LANES)


def _mlstm_kernel(q_ref, k_ref, v_ref, o_ref, zg_ref, bg_ref, mg_ref, C0_ref, n0_ref, m0_ref,
                  hm_ref, C_ref, n_ref, m_ref, *, heads, dk, dv, chunk, nsub):
    L = chunk

    @pl.when(pl.program_id(1) == 0)
    def _():
        C_ref[...] = C0_ref[0]
        n_ref[...] = n0_ref[0]
        m_ref[...] = m0_ref[0]

    tri = _lower_tri(L, BF16)
    causal = (lax.broadcasted_iota(jnp.int32, (L, L), 0) >= lax.broadcasted_iota(jnp.int32, (L, L), 1))
    for s in range(nsub):
        for bb in range(q_ref.shape[0]):
            _mlstm_chunk(bb, slice(s * L, (s + 1) * L), tri, causal, q_ref, k_ref, v_ref, o_ref, zg_ref, bg_ref,
                         mg_ref, hm_ref, C_ref, n_ref, m_ref, heads=heads, dk=dk, dv=dv, L=L)


def _mlstm_chunk(bb, r, tri, causal, q_ref, k_ref, v_ref, o_ref, zg_ref, bg_ref, mg_ref,
                 hm_ref, C_ref, n_ref, m_ref, *, heads, dk, dv, L):
    gates = zg_ref[bb, r, :] + bg_ref[...]
    bsum = _cumsum_rows(_log_sigmoid(gates), tri)
    lane = lax.broadcasted_iota(jnp.int32, gates.shape, 1)
    rows = jnp.where(lane < heads, gates, bsum).T

    for h in range(heads):
        i_col = gates[:, h:h + 1]
        b_col = bsum[:, heads + h:heads + h + 1]
        i_row = rows[h:h + 1, :]
        b_row = rows[heads + h:heads + h + 1, :]
        m_prev = m_ref[bb, h:h + 1, 0:1]
        n_prev = n_ref[bb, h:h + 1, :]
        C_prev = C_ref[bb, h]

        q = q_ref[bb, r, h * dk:(h + 1) * dk] * (dk ** -0.5)
        k = k_ref[bb, r, h * dk:(h + 1) * dk]
        vb = v_ref[bb, r, h * dv:(h + 1) * dv].astype(BF16)
        qb = q.astype(BF16)

        D = jnp.where(causal, (b_col - b_row) + i_row, NEG_BIG)
        inter = b_col + m_prev
        mt = jnp.maximum(inter, jnp.max(D, axis=-1, keepdims=True))
        a_inter = jnp.exp(inter - mt)
        S = lax.dot_general(qb, k.astype(BF16), (((1,), (1,)), ((), ())),
                            preferred_element_type=F32) * jnp.exp(D - mt)
        num = a_inter * jnp.dot(qb, C_prev.astype(BF16), preferred_element_type=F32) \
            + jnp.dot(S.astype(BF16), vb, preferred_element_type=F32)
        den = a_inter * jnp.sum(q * n_prev, axis=-1, keepdims=True) + jnp.sum(S, axis=-1, keepdims=True)
        hh = num / jnp.maximum(jnp.abs(den), jnp.exp(-mt))

        b_last = b_col[L - 1:L, :]
        wlast = (b_last - b_col) + i_col
        m_new = jnp.maximum(b_last + m_prev, jnp.max(wlast, axis=0, keepdims=True))
        a_c = jnp.exp((b_last + m_prev) - m_new)
        kw = k * jnp.exp(wlast - m_new)
        C_ref[bb, h] = a_c * C_prev + lax.dot_general(kw.astype(BF16), vb, (((0,), (0,)), ((), ())),
                                                     preferred_element_type=F32)
        n_ref[bb, h:h + 1, :] = a_c * n_prev + jnp.sum(kw, axis=0, keepdims=True)
        m_ref[bb, h:h + 1, :] = jnp.broadcast_to(m_new, (1, LANES))

        ms = jnp.mean(hh * hh, axis=-1, keepdims=True)
        y = (hh * lax.rsqrt(ms + EPS)) * mg_ref[:, h * dv:(h + 1) * dv]
        y = y * _sigmoid(o_ref[bb, r, h * dv:(h + 1) * dv])
        hm_ref[bb, r, h * dv:(h + 1) * dv] = y.astype(BF16)


def _mlstm(z, zg, b_gate, norm_g, C0, n0, m0, sl):
    B, T, _ = z.shape
    _, _, heads, dk, dv = C0.shape
    QW, VW = heads * dk, heads * dv
    assert VW == 2 * QW and 2 * heads <= LANES
    L = MLSTM_CHUNK if T % MLSTM_CHUNK == 0 else T
    Ls = MLSTM_STEP if (T % MLSTM_STEP == 0 and MLSTM_STEP % L == 0) else L
    BB = MLSTM_BATCH if B % MLSTM_BATCH == 0 else 1
    bias = jnp.zeros((1, LANES), F32).at[0, :2 * heads].set(b_gate)
    m0b = jnp.broadcast_to(m0[..., None], m0.shape + (LANES,))
    vmem = BB * (2 * (Ls * (2 * QW + 2 * VW + LANES) * 4 + Ls * VW * 2) + 4 * heads * dk * dv * 4) + 12 * MIB
    kern = functools.partial(_mlstm_kernel, heads=heads, dk=dk, dv=dv, chunk=L, nsub=Ls // L)
    hm, C, n, m = pl.pallas_call(
        kern,
        grid=(B // BB, T // Ls),
        in_specs=[
            pl.BlockSpec((BB, Ls, QW), lambda b, c: (b, c, 0)),
            pl.BlockSpec((BB, Ls, QW), lambda b, c: (b, c, 1)),
            pl.BlockSpec((BB, Ls, VW), lambda b, c: (b, c, 1)),
            pl.BlockSpec((BB, Ls, VW), lambda b, c: (b, c, 2)),
            pl.BlockSpec((BB, Ls, LANES), lambda b, c: (b, c, 0)),
            pl.BlockSpec((1, LANES), lambda b, c: (0, 0)),
            pl.BlockSpec((1, VW), lambda b, c: (0, 0)),
            pl.BlockSpec((1, BB, heads, dk, dv), lambda b, c: (sl, b, 0, 0, 0)),
            pl.BlockSpec((1, BB, heads, dk), lambda b, c: (sl, b, 0, 0)),
            pl.BlockSpec((1, BB, heads, LANES), lambda b, c: (sl, b, 0, 0)),
        ],
        out_specs=[
            pl.BlockSpec((BB, Ls, VW), lambda b, c: (b, c, 0)),
            pl.BlockSpec((BB, heads, dk, dv), lambda b, c: (b, 0, 0, 0)),
            pl.BlockSpec((BB, heads, dk), lambda b, c: (b, 0, 0)),
            pl.BlockSpec((BB, heads, LANES), lambda b, c: (b, 0, 0)),
        ],
        out_shape=[
            jax.ShapeDtypeStruct((B, T, VW), BF16),
            jax.ShapeDtypeStruct((B, heads, dk, dv), F32),
            jax.ShapeDtypeStruct((B, heads, dk), F32),
            jax.ShapeDtypeStruct((B, heads, LANES), F32),
        ],
        compiler_params=_params(("parallel", "arbitrary"), vmem),
        name="mlstm",
    )(z, z, z, z, zg, bias, norm_g.reshape(1, VW), C0, n0, m0b)
    return hm, C, n, m[:, :, 0]


def _hgrn_tables(nb):
    u = np.arange(nb)[:, None]
    j = np.arange(nb)[None, :]
    mats, masks = [j <= u], []
    w = nb // 2
    while w >= 1:
        r = (u // (2 * w)) * (2 * w) + w - 1
        upper = (u % (2 * w)) >= w
        mats.append(np.where(upper, (j > r) & (j <= u), (j > u) & (j <= r)))
        masks.append(((u // (2 * w)) == (j // (2 * w))) & upper & ((j % (2 * w)) < w))
        w //= 2
    mats.append(j > u)
    masks.append(u == j)
    masks = np.stack(masks).astype(np.float32)
    return np.concatenate(mats, 0).astype(np.float32), np.concatenate([masks, masks], axis=2)


def _hgrn_kernel(*refs, heads, dk, dv, layer, chunk, ncast):
    q_ref, f_ref, i_ref, g_ref, lb_ref, ng_ref, S0_ref, mst_ref, lvl_ref = refs[:9]
    oh_ref, S_ref = refs[9 + ncast:11 + ncast]
    st_scr = refs[-1]
    nb = HGRN_BLOCK
    nlev = lvl_ref.shape[0]
    c = pl.program_id(1)

    for src, dst in zip(refs[9:9 + ncast], refs[11 + ncast:11 + 2 * ncast]):
        dst[...] = src[...].astype(BF16)

    @pl.when(c == 0)
    def _():
        for h in range(heads):
            st_scr[h] = S0_ref[0, 0, h].T

    lbs = lb_ref[...]
    e = jnp.exp(lbs - jnp.max(lbs, axis=0, keepdims=True))
    sm = e / jnp.sum(e, axis=0, keepdims=True)
    cum = sm[0:1, :]
    for j in range(1, layer + 1):
        cum = cum + sm[j:j + 1, :]
    lb = cum - sm[0:1, :]
    lb_floor = jnp.maximum(lb, LB_FLOOR)
    one_m = 1.0 - lb

    mst = mst_ref[...]
    masks = [lvl_ref[i] > 0.5 for i in range(nlev)]
    nt = (((1,), (1,)), ((), ()))
    first = lax.broadcasted_iota(jnp.int32, (nb, 2 * dk), 1) < dk

    def pair_diag(a):
        zero = jnp.zeros_like(a)
        return jnp.concatenate([jnp.where(first, a, zero), jnp.where(first, zero, a)], axis=0)

    def block(j, carry):
        rows = pl.ds(j * nb, nb)
        fx = f_ref[0, rows, :]
        en = jnp.exp(-jnp.abs(fx))
        rc = 1.0 / (1.0 + en)
        er = en * rc
        pos = fx >= 0.0
        g = jnp.log(lb_floor + one_m * jnp.where(pos, rc, er))
        kk = one_m * jnp.where(pos, er, rc)
        qs = _silu(q_ref[0, rows, :])
        vb = i_ref[0, rows, :].astype(BF16)
        gate = _silu(g_ref[0, rows, :])

        ghi = g.astype(BF16)
        gmid = (g - ghi.astype(F32)).astype(BF16)
        E = jnp.exp(jnp.dot(mst, ghi, preferred_element_type=F32)
                    + jnp.dot(mst, gmid, preferred_element_type=F32))
        EG = E[:nb]
        qG = (qs * EG).astype(BF16)
        kL = (kk * E[nlev * nb:]).astype(BF16)
        Ql = [(qs * E[(i + 1) * nb:(i + 2) * nb]).astype(BF16) for i in range(nlev - 1)] + [qs.astype(BF16)]
        Kl = [(kk * E[(i + 1) * nb:(i + 2) * nb]).astype(BF16) for i in range(nlev - 1)] + [kk.astype(BF16)]
        decay = EG[nb - 1:nb]

        pairs = [slice(2 * p * dk, (2 * p + 2) * dk) for p in range(heads // 2)]
        A2 = []
        for sl2 in pairs:
            a = jnp.zeros((nb, 2 * nb), F32)
            for i in range(nlev):
                a = jnp.where(masks[i], lax.dot_general(Ql[i][:, sl2], pair_diag(Kl[i][:, sl2]), nt,
                                                        preferred_element_type=F32), a)
            A2.append(a.astype(BF16))
        o_intra = jnp.concatenate(
            [jnp.dot(a, pair_diag(vb[:, sl2]), preferred_element_type=F32) for a, sl2 in zip(A2, pairs)], axis=1)

        for h in range(heads):
            sl = slice(h * dk, (h + 1) * dk)
            st = st_scr[h]
            o = lax.dot_general(qG[:, sl], st.astype(BF16), nt, preferred_element_type=F32)
            o = o + o_intra[:, sl]
            upd = lax.dot_general(vb[:, sl], kL[:, sl], (((0,), (0,)), ((), ())),
                                  preferred_element_type=F32)
            st_scr[h] = st * decay[:, sl] + upd

            ms = jnp.mean(o * o, axis=-1, keepdims=True)
            y = (o * lax.rsqrt(ms + EPS)) * ng_ref[:, sl] * gate[:, sl]
            oh_ref[0, rows, sl] = y.astype(BF16)
        return carry

    for j in range(chunk // nb):
        block(j, 0)

    @pl.when(c == pl.num_programs(1) - 1)
    def _():
        for h in range(heads):
            S_ref[0, h] = st_scr[h].T


def _hgrn_steps(B, T):
    Lc = HGRN_CHUNK if T % HGRN_CHUNK == 0 else T
    return Lc, B * (T // Lc)


def _hgrn_can_cast(B, T, weights):
    _, steps = _hgrn_steps(B, T)
    return all(w.shape[1] % (steps * BF16_ROWS) == 0 for w in weights)


def _hgrn(z, lower_bounds, layer, norm_g, S0, sl, col_block, cast=()):
    B, T, _ = z.shape
    _, _, heads, dk, dv = S0.shape
    HW = heads * dk
    assert dk == dv
    Lc, steps = _hgrn_steps(B, T)
    NC = T // Lc
    assert Lc % HGRN_BLOCK == 0
    depth = lower_bounds.shape[0]
    mst, lvl = _hgrn_tables(HGRN_BLOCK)
    mst = jnp.asarray(mst, BF16)
    lvl = jnp.asarray(lvl, F32)
    slabs = [(w.shape[1] // steps, w.shape[2]) for w in cast]
    vmem = 2 * (4 * Lc * HW * 4 + Lc * HW * 2) + 5 * heads * dk * dv * 4 \
        + 6 * mst.shape[0] * HW * 4 + sum(2 * rb * cols * 6 for rb, cols in slabs) + 16 * MIB
    kern = functools.partial(_hgrn_kernel, heads=heads, dk=dk, dv=dv, layer=layer, chunk=Lc, ncast=len(cast))
    zspec = lambda idx: pl.BlockSpec((1, Lc, HW), lambda b, c: (b, c, idx))
    outs = pl.pallas_call(
        kern,
        grid=(B, NC),
        in_specs=[
            zspec(col_block), zspec(col_block + 1), zspec(col_block + 2), zspec(col_block + 3),
            pl.BlockSpec((depth, HW), lambda b, c: (0, 0)),
            pl.BlockSpec((1, HW), lambda b, c: (0, 0)),
            pl.BlockSpec((1, 1, heads, dk, dv), lambda b, c: (sl, b, 0, 0, 0)),
            pl.BlockSpec(mst.shape, lambda b, c: (0, 0)),
            pl.BlockSpec(lvl.shape, lambda b, c: (0, 0, 0)),
        ] + [pl.BlockSpec((None, rb, cols), lambda b, c: (layer, b * NC + c, 0)) for rb, cols in slabs],
        out_specs=[
            pl.BlockSpec((1, Lc, HW), lambda b, c: (b, c, 0)),
            pl.BlockSpec((1, heads, dk, dv), lambda b, c: (b, 0, 0, 0)),
        ] + [pl.BlockSpec((rb, cols), lambda b, c: (b * NC + c, 0)) for rb, cols in slabs],
        out_shape=[
            jax.ShapeDtypeStruct((B, T, HW), BF16),
            jax.ShapeDtypeStruct((B, heads, dk, dv), F32),
        ] + [jax.ShapeDtypeStruct(w.shape[1:], BF16) for w in cast],
        scratch_shapes=[pltpu.VMEM((heads, dv, dk), F32)],
        compiler_params=_params(("parallel", "arbitrary"), vmem),
        name="hgrn",
    )(z, z, z, z, lower_bounds, norm_g.reshape(1, HW), S0, mst, lvl, *cast)
    return outs[0], outs[1], list(outs[2:])


def _out_kernel(hm_ref, oh_ref, wa_ref, wb_ref, x_ref, ga_ref, xo_ref):
    acc = jnp.dot(hm_ref[...], wa_ref[...], preferred_element_type=F32)
    acc = acc + jnp.dot(oh_ref[...], wb_ref[...], preferred_element_type=F32)
    xo_ref[...] = x_ref[...] + ga_ref[...] * acc.reshape(xo_ref.shape)


def _out_proj(hm, oh, w_out, layer, x, mod, row0, *, tm=1024, tn=1024):
    B, T, D = x.shape
    MW, HW = hm.shape[-1], oh.shape[-1]
    assert MW == HW and w_out.shape[-2] == MW + HW
    G, R, TG, reps = _row_groups(B, T, tm)
    TM, M = TG * R, B * T
    tn = _col_tile(D, tn)
    vmem = 2 * (2 * TM * MW * 2 + 2 * MW * tn * 2 + 2 * TM * tn * 4 + tn * 4) + 2 * TM * tn * 4 + 4 * MIB
    return pl.pallas_call(
        _out_kernel,
        grid=(G // TG, D // tn),
        in_specs=[
            pl.BlockSpec((TM, MW), lambda i, j: (i, 0)),
            pl.BlockSpec((TM, HW), lambda i, j: (i, 0)),
            _weight_spec(w_out, layer, (MW, tn), lambda i, j: (0, j)),
            _weight_spec(w_out, layer, (HW, tn), lambda i, j: (1, j)),
            pl.BlockSpec((TG, R, tn), lambda i, j: (i, 0, j)),
            _mod_spec(mod, layer, row0, 2, TG, reps, tn, once=False),
        ],
        out_specs=pl.BlockSpec((TG, R, tn), lambda i, j: (i, 0, j)),
        out_shape=jax.ShapeDtypeStruct((G, R, D), F32),
        compiler_params=_params(("parallel", "arbitrary"), vmem),
        name="out_proj",
    )(hm.reshape(M, MW), oh.reshape(M, HW), w_out, w_out, x.reshape(G, R, D), mod).reshape(B, T, D)


def _ffn_kernel(x_hbm, sc_ref, sh_ref, g_ref, ga_ref, wu_ref, wd_ref, fg_ref, o_ref, h_scr, x_sem,
                *, final, n_split, f_split):
    i, f = pl.program_id(0), pl.program_id(1)
    TG, R, D = o_ref.shape
    dn = D // n_split
    ts = wu_ref.shape[-1] // f_split

    @pl.when(f == 0)
    def _():
        x_copy = pltpu.make_async_copy(x_hbm.at[pl.ds(i * TG, TG)], o_ref, x_sem)
        x_copy.start()
        x_copy.wait()

        def piece(gs, rs, fs):
            h = _mod_norm(o_ref[gs, rs, :], g_ref[...], sc_ref[gs], sh_ref[gs])
            h_scr[fs, :] = h.reshape(-1, D).astype(BF16)

        _for_row_chunks(TG, R, piece)

    blk = (TG, R, dn)
    for k in range(f_split):
        u = jnp.dot(h_scr[...], wu_ref[:, k * ts:(k + 1) * ts], preferred_element_type=F32)
        a = jnp.square(jnp.maximum(u, 0.0)).astype(BF16)
        for n in range(n_split):
            sl = slice(n * dn, (n + 1) * dn)
            part = jnp.dot(a, wd_ref[k * ts:(k + 1) * ts, sl], preferred_element_type=F32)
            o_ref[:, :, sl] += ga_ref[:, :, sl] * part.reshape(blk)

    if final:
        @pl.when(f == pl.num_programs(1) - 1)
        def _():
            def piece(gs, rs, fs):
                xo = o_ref[gs, rs, :]
                ms = jnp.mean(xo * xo, axis=-1, keepdims=True)
                o_ref[gs, rs, :] = (xo * lax.rsqrt(ms + EPS)) * fg_ref[...]

            _for_row_chunks(TG, R, piece)


def _ffn(x, mod, row0, g, w_up, w_down, layer, final_g, *, final, tm=512, tf=1024):
    B, T, D = x.shape
    FF = w_up.shape[-1]
    G, R, TG, reps = _row_groups(B, T, tm)
    TM = TG * R
    tf = _col_tile(FF, tf)
    n_split = 4 if D % (4 * LANES) == 0 else 1
    f_split = 1
    vmem = 2 * TM * D * 4 + 2 * 2 * D * tf * 2 + (3 * TG + 4) * SUBLANES * D * 4 + TM * D * 2 + TM * tf * 6 \
        + 2 * TM * (D // n_split) * 4 + 4 * MIB
    kern = functools.partial(_ffn_kernel, final=final, n_split=n_split, f_split=f_split)
    once = pl.Buffered(1)
    vec = lambda part: _mod_spec(mod, layer, row0, part, TG, reps, D)
    return pl.pallas_call(
        kern,
        grid=(G // TG, FF // tf),
        in_specs=[
            pl.BlockSpec(memory_space=pl.ANY),
            vec(4), vec(3),
            pl.BlockSpec((None, 1, D), lambda i, f: (layer, 0, 0), pipeline_mode=once),
            vec(5),
            _weight_spec(w_up, layer, (D, tf), lambda i, f: (0, f)),
            _weight_spec(w_down, layer, (tf, D), lambda i, f: (f, 0)),
            pl.BlockSpec((1, D), lambda i, f: (0, 0), pipeline_mode=once),
        ],
        out_specs=pl.BlockSpec((TG, R, D), lambda i, f: (i, 0, 0)),
        out_shape=jax.ShapeDtypeStruct((G, R, D), F32),
        scratch_shapes=[pltpu.VMEM((TM, D), BF16), pltpu.SemaphoreType.DMA(())],
        compiler_params=_params(("parallel", "arbitrary"), vmem),
        name="ffn",
    )(x.reshape(G, R, D), mod, mod, g.reshape(-1, 1, D), mod, w_up, w_down,
      final_g.reshape(1, D)).reshape(B, T, D)


def _trunk(x, mod, row0, C0, n0, m0, S0, per_layer_state, wts, late=None):
    depth = mod.shape[0]
    made = []
    Cs, ns, ms, Ss = [], [], [], []
    for l in range(depth):
        sl = l if per_layer_state else 0
        z, zg = _in_proj(x, mod, row0, wts["norm1_g"], wts["w_main"], wts["w_gate"], l)
        hm, C, n, m = _mlstm(z, zg, wts["b_gate"][l], wts["mlstm_norm_g"][l], C0, n0, m0, sl)
        oh, S, cast = _hgrn(z, wts["lower_bounds"], l, wts["hgrn_norm_g"][l], S0, sl, col_block=3,
                            cast=wts["late_f32"] if late is None else ())
        w_out, w_up, w_down = cast if late is None else late[l]
        made.append((w_out, w_up, w_down))
        x = _out_proj(hm, oh, w_out, l, x, mod, row0)
        x = _ffn(x, mod, row0, wts["norm2_g"], w_up, w_down, l, wts["final_g"], final=(l == depth - 1))
        Cs.append(C); ns.append(n); ms.append(m); Ss.append(S)
    return x, jnp.stack(Cs), jnp.stack(ns), jnp.stack(ms), jnp.stack(Ss), made


def kernel(x_prompt, x_sample, state_mlstm_C, state_mlstm_n, state_mlstm_m, state_hgrn_S, c_prompt, c_sample, w_mod, b_mod, norm1_g, w_in, b_gate, lower_bounds, mlstm_norm_g, hgrn_norm_g, w_out, norm2_g, w_up, w_down, final_g):
    depth = w_mod.shape[0]
    Bp, Bs = x_prompt.shape[0], x_sample.shape[0]
    _, _, heads, dk, dv = state_mlstm_C.shape
    _, _, hh, hdk, hdv = state_hgrn_S.shape
    gate0 = 2 * heads * dk + 2 * heads * dv
    ngate = 2 * heads

    w_main, w_gate = _prep_w_in(w_in, gate0, ngate)
    wts = dict(norm1_g=norm1_g, w_main=w_main, w_gate=w_gate, b_gate=b_gate, lower_bounds=lower_bounds,
               mlstm_norm_g=mlstm_norm_g, hgrn_norm_g=hgrn_norm_g, norm2_g=norm2_g, final_g=final_g,
               late_f32=(w_out, w_up, w_down))
    late = None
    if not _hgrn_can_cast(*x_prompt.shape[:2], wts["late_f32"]):
        late = [(w_out.astype(BF16), w_up.astype(BF16), w_down.astype(BF16))] * depth

    pad_rows = lambda c: jnp.pad(c, ((0, -c.shape[0] % BF16_ROWS), (0, 0)))
    c_all = jnp.concatenate([pad_rows(c_prompt), pad_rows(c_sample)], axis=0)
    row_s = Bp + (-Bp % BF16_ROWS)
    mod = _modulation(c_all, w_mod, b_mod)
    mod = mod.reshape(depth, c_all.shape[0], 1, mod.shape[-1])

    zC = jnp.zeros((1, Bp, heads, dk, dv), F32)
    zn = jnp.zeros((1, Bp, heads, dk), F32)
    zm = jnp.zeros((1, Bp, heads), F32)
    zS = jnp.zeros((1, Bp, hh, hdk, hdv), F32)
    y_p, pC, pn, pm, pS, late = _trunk(x_prompt, mod, 0, zC, zn, zm, zS, False, wts, late)
    y_s, sC, sn, sm, sS, _ = _trunk(x_sample, mod, row_s, state_mlstm_C, state_mlstm_n, state_mlstm_m,
                                    state_hgrn_S, True, wts, late)
    return (y_p, y_s, pC, pn, pm, pS, sC, sn, sm, sS)
```

```python
import functools

import jax
import jax.numpy as jnp
import numpy as np
from jax import lax
from jax.experimental import pallas as pl
from jax.experimental.pallas import tpu as pltpu

EPS = 1e-6
NEG_BIG = -1e30
LB_FLOOR = 1e-30
MLSTM_CHUNK = 256
MLSTM_STEP = 512
MLSTM_BATCH = 1
HGRN_BLOCK = 16
HGRN_CHUNK = 128
LANES = 128
SUBLANES = 8
BF16_ROWS = 16
MIB = 1024 * 1024
VMEM_CAP = 63 * MIB

F32 = jnp.float32
BF16 = jnp.bfloat16


def _params(semantics, vmem_bytes):
    return pltpu.CompilerParams(dimension_semantics=semantics,
                                vmem_limit_bytes=int(min(VMEM_CAP, vmem_bytes)))


def _sigmoid(x):
    return 1.0 / (1.0 + jnp.exp(-x))


def _silu(x):
    return x * _sigmoid(x)


def _log_sigmoid(x):
    return jnp.minimum(x, 0.0) - jnp.log1p(jnp.exp(-jnp.abs(x)))


def _lower_tri(n, dtype):
    r = lax.broadcasted_iota(jnp.int32, (n, n), 0)
    c = lax.broadcasted_iota(jnp.int32, (n, n), 1)
    return jnp.where(r >= c, 1.0, 0.0).astype(dtype)


def _cumsum_rows(x, tri):
    hi = x.astype(BF16)
    r1 = x - hi.astype(F32)
    mid = r1.astype(BF16)
    lo = (r1 - mid.astype(F32)).astype(BF16)
    dot = functools.partial(jnp.dot, preferred_element_type=F32)
    return (dot(tri, hi) + dot(tri, mid)) + dot(tri, lo)


def _col_tile(n, pref):
    return max(t for t in range(LANES, min(pref, n) + 1, LANES) if n % t == 0)


def _row_groups(B, T, tm_pref):
    if T >= tm_pref:
        assert T % tm_pref == 0
        R, TG = tm_pref, 1
    else:
        R = T
        TG = max(1, min(B, tm_pref // T))
        assert B % TG == 0
    return (B * T) // R, R, TG, T // R


def _for_row_chunks(TG, R, body, step=4 * BF16_ROWS):
    step = step if R % step == 0 else R
    per = R // step

    def piece(c, carry):
        t = c // per
        r = pl.multiple_of((c % per) * step, step)
        body(pl.ds(t, 1), pl.ds(r, step), pl.ds(pl.multiple_of(t * R + r, step), step))
        return carry

    lax.fori_loop(0, TG * per, piece, 0)


def _mod_spec(mod, layer, row0, part, TG, reps, width, once=True):
    D = mod.shape[-1] // 6
    per = D // width
    mode = dict(pipeline_mode=pl.Buffered(1)) if once else {}
    col = (lambda j: part) if per == 1 else (lambda j: part * per + j)
    if reps > 1:
        assert TG == 1
        return pl.BlockSpec((None, 1, 1, width), lambda i, j: (layer, row0 + i // reps, 0, col(j)), **mode)
    assert row0 % TG == 0
    return pl.BlockSpec((None, TG, 1, width), lambda i, j: (layer, row0 // TG + i, 0, col(j)), **mode)


def _weight_spec(w, layer, block, index):
    if w.ndim == 2:
        return pl.BlockSpec(block, index)
    return pl.BlockSpec((None,) + block, lambda i, j: (layer,) + index(i, j))


def _mod_norm(x, g, sc, sh):
    ms = jnp.mean(x * x, axis=-1, keepdims=True)
    return (x * lax.rsqrt(ms + EPS) * g) * (1.0 + sc) + sh


def _mod_kernel(c_ref, w_ref, b_ref, o_ref):
    c = c_ref[...]
    a = _silu(c).astype(BF16)
    w = w_ref[0].astype(BF16)
    o_ref[0] = jnp.dot(a, w, preferred_element_type=F32) + b_ref[0]


def _modulation(c_all, w_mod, b_mod, *, tn=512):
    depth, D, N = w_mod.shape
    MP = c_all.shape[0]
    tn = _col_tile(N, tn)
    vmem = 2 * (D * tn * 4 + MP * tn * 4 + tn * 4) + 2 * MP * D * 4 + D * tn * 2 + 4 * MIB
    return pl.pallas_call(
        _mod_kernel,
        grid=(depth, N // tn),
        in_specs=[
            pl.BlockSpec((MP, D), lambda l, j: (0, 0)),
            pl.BlockSpec((1, D, tn), lambda l, j: (l, 0, j)),
            pl.BlockSpec((1, 1, tn), lambda l, j: (l, 0, j)),
        ],
        out_specs=pl.BlockSpec((1, MP, tn), lambda l, j: (l, 0, j)),
        out_shape=jax.ShapeDtypeStruct((depth, MP, N), F32),
        compiler_params=_params(("parallel", "parallel"), vmem),
        name="modulation",
    )(c_all, w_mod, b_mod.reshape(depth, 1, N))


def _win_kernel(w_ref, o_ref):
    o_ref[...] = w_ref[0].astype(BF16)


def _prep_w_in(w_in, gate0, ngate, *, rows=512):
    depth, D, NC = w_in.shape
    NZ = NC - ngate
    wT = jnp.swapaxes(w_in, 1, 2)
    gate = jnp.pad(wT[:, gate0:gate0 + ngate, :], ((0, 0), (0, LANES - ngate), (0, 0))).astype(BF16)
    if (gate0 + ngate) % SUBLANES:
        return jnp.concatenate([wT[:, :gate0], wT[:, gate0 + ngate:]], axis=1).astype(BF16), gate
    rows = max(r for r in range(BF16_ROWS, rows + 1, BF16_ROWS) if gate0 % r == 0 and NZ % r == 0)
    src_row = lambda c: pl.multiple_of(jnp.where(c * rows < gate0, c * rows, c * rows + ngate), SUBLANES)
    main = pl.pallas_call(
        _win_kernel,
        grid=(depth, NZ // rows),
        in_specs=[pl.BlockSpec((pl.Element(1), pl.Element(rows), pl.Element(D)), lambda l, c: (l, src_row(c), 0))],
        out_specs=pl.BlockSpec((None, rows, D), lambda l, c: (l, c, 0)),
        out_shape=jax.ShapeDtypeStruct((depth, NZ, D), BF16),
        compiler_params=_params(("parallel", "parallel"), 2 * rows * D * 6 + rows * D * 4 + 4 * MIB),
        name="w_in_prep",
    )(wT)
    return main, gate


def _in_kernel(x_ref, sc_ref, sh_ref, g_ref, w_ref, wg_ref, z_ref, zg_ref, h_scr):
    nt = (((1,), (1,)), ((), ()))

    @pl.when(pl.program_id(1) == 0)
    def _():
        def piece(gs, rs, fs):
            h = _mod_norm(x_ref[gs, rs, :], g_ref[...], sc_ref[gs], sh_ref[gs])
            hb = h.reshape(-1, h.shape[-1]).astype(BF16)
            h_scr[fs, :] = hb
            zg_ref[fs, :] = lax.dot_general(hb, wg_ref[...], nt, preferred_element_type=F32)

        _for_row_chunks(*x_ref.shape[:2], piece)

    z_ref[...] = lax.dot_general(h_scr[...], w_ref[...], nt, preferred_element_type=F32)


def _in_proj(x, mod, row0, g, w_main, w_gate, layer, *, tm=512, tn=1024):
    B, T, D = x.shape
    NZ = w_main.shape[1]
    G, R, TG, reps = _row_groups(B, T, tm)
    TM, M = TG * R, B * T
    tn = _col_tile(NZ, tn)
    vmem = 2 * TM * D * 4 + D * LANES * 2 + (2 * TG + 1) * SUBLANES * D * 4 \
        + 2 * (D * tn * 2 + TM * tn * 4 + TM * LANES * 4) + TM * D * 2 + 8 * MIB
    once = pl.Buffered(1)
    z, zg = pl.pallas_call(
        _in_kernel,
        grid=(G // TG, NZ // tn),
        in_specs=[
            pl.BlockSpec((TG, R, D), lambda i, j: (i, 0, 0)),
            _mod_spec(mod, layer, row0, 1, TG, reps, D),
            _mod_spec(mod, layer, row0, 0, TG, reps, D),
            pl.BlockSpec((None, 1, D), lambda i, j: (layer, 0, 0), pipeline_mode=once),
            pl.BlockSpec((None, tn, D), lambda i, j: (layer, j, 0)),
            pl.BlockSpec((None, LANES, D), lambda i, j: (layer, 0, 0), pipeline_mode=once),
        ],
        out_specs=[
            pl.BlockSpec((TM, tn), lambda i, j: (i, j)),
            pl.BlockSpec((TM, LANES), lambda i, j: (i, 0)),
        ],
        out_shape=[jax.ShapeDtypeStruct((M, NZ), F32), jax.ShapeDtypeStruct((M, LANES), F32)],
        scratch_shapes=[pltpu.VMEM((TM, D), BF16)],
        compiler_params=_params(("parallel", "arbitrary"), vmem),
        name="in_proj",
    )(x.reshape(G, R, D), mod, mod, g.reshape(-1, 1, D), w_main, w_gate)
    return z.reshape(B, T, NZ), zg.reshape(B, T, LANES)


def _mlstm_kernel(*refs, heads, dk, dv, chunk, nsub, fresh):
    q_ref, k_ref, v_ref, o_ref, zg_ref, bg_ref, mg_ref = refs[:7]
    hm_ref, C_ref, n_ref, m_ref = refs[-4:]
    L = chunk

    @pl.when(pl.program_id(1) == 0)
    def _():
        if fresh:
            C_ref[...] = jnp.zeros_like(C_ref)
            n_ref[...] = jnp.zeros_like(n_ref)
            m_ref[...] = jnp.zeros_like(m_ref)
        else:
            C0_ref, n0_ref, m0_ref = refs[7:10]
            C_ref[...] = C0_ref[0]
            n_ref[...] = n0_ref[0]
            m_ref[...] = m0_ref[0]

    tri = _lower_tri(L, BF16)
    causal = (lax.broadcasted_iota(jnp.int32, (L, L), 0) >= lax.broadcasted_iota(jnp.int32, (L, L), 1))
    for s in range(nsub):
        for bb in range(q_ref.shape[0]):
            _mlstm_chunk(bb, slice(s * L, (s + 1) * L), tri, causal, q_ref, k_ref, v_ref, o_ref, zg_ref, bg_ref,
                         mg_ref, hm_ref, C_ref, n_ref, m_ref, heads=heads, dk=dk, dv=dv, L=L)


def _mlstm_chunk(bb, r, tri, causal, q_ref, k_ref, v_ref, o_ref, zg_ref, bg_ref, mg_ref,
                 hm_ref, C_ref, n_ref, m_ref, *, heads, dk, dv, L):
    gates = zg_ref[bb, r, :] + bg_ref[...]
    bsum = _cumsum_rows(_log_sigmoid(gates), tri)
    lane = lax.broadcasted_iota(jnp.int32, gates.shape, 1)
    rows = jnp.where(lane < heads, gates, bsum).T

    for h in range(heads):
        i_col = gates[:, h:h + 1]
        b_col = bsum[:, heads + h:heads + h + 1]
        i_row = rows[h:h + 1, :]
        b_row = rows[heads + h:heads + h + 1, :]
        m_prev = m_ref[bb, h:h + 1, 0:1]
        n_prev = n_ref[bb, h:h + 1, :]
        C_prev = C_ref[bb, h]

        q = q_ref[bb, r, h * dk:(h + 1) * dk] * (dk ** -0.5)
        k = k_ref[bb, r, h * dk:(h + 1) * dk]
        vb = v_ref[bb, r, h * dv:(h + 1) * dv].astype(BF16)
        qb = q.astype(BF16)

        D = jnp.where(causal, (b_col - b_row) + i_row, NEG_BIG)
        inter = b_col + m_prev
        mt = jnp.maximum(inter, jnp.max(D, axis=-1, keepdims=True))
        a_inter = jnp.exp(inter - mt)
        S = lax.dot_general(qb, k.astype(BF16), (((1,), (1,)), ((), ())),
                            preferred_element_type=F32) * jnp.exp(D - mt)
        num = a_inter * jnp.dot(qb, C_prev.astype(BF16), preferred_element_type=F32) \
            + jnp.dot(S.astype(BF16), vb, preferred_element_type=F32)
        den = a_inter * jnp.sum(q * n_prev, axis=-1, keepdims=True) + jnp.sum(S, axis=-1, keepdims=True)
        hh = num / jnp.maximum(jnp.abs(den), jnp.exp(-mt))

        b_last = b_col[L - 1:L, :]
        wlast = (b_last - b_col) + i_col
        m_new = jnp.maximum(b_last + m_prev, jnp.max(wlast, axis=0, keepdims=True))
        a_c = jnp.exp((b_last + m_prev) - m_new)
        kw = k * jnp.exp(wlast - m_new)
        C_ref[bb, h] = a_c * C_prev + lax.dot_general(kw.astype(BF16), vb, (((0,), (0,)), ((), ())),
                                                     preferred_element_type=F32)
        n_ref[bb, h:h + 1, :] = a_c * n_prev + jnp.sum(kw, axis=0, keepdims=True)
        m_ref[bb, h:h + 1, :] = jnp.broadcast_to(m_new, (1, LANES))

        ms = jnp.mean(hh * hh, axis=-1, keepdims=True)
        y = (hh * lax.rsqrt(ms + EPS)) * mg_ref[:, h * dv:(h + 1) * dv]
        y = y * _sigmoid(o_ref[bb, r, h * dv:(h + 1) * dv])
        hm_ref[bb, r, h * dv:(h + 1) * dv] = y.astype(BF16)


def _mlstm(z, zg, b_gate, norm_g, dims, state=None):
    B, T, _ = z.shape
    heads, dk, dv = dims
    QW, VW = heads * dk, heads * dv
    assert VW == 2 * QW and 2 * heads <= LANES
    L = MLSTM_CHUNK if T % MLSTM_CHUNK == 0 else T
    Ls = MLSTM_STEP if (T % MLSTM_STEP == 0 and MLSTM_STEP % L == 0) else L
    BB = MLSTM_BATCH if B % MLSTM_BATCH == 0 else 1
    bias = jnp.zeros((1, LANES), F32).at[0, :2 * heads].set(b_gate)
    state_specs, state_args = [], []
    if state is not None:
        C0, n0, m0, sl = state
        state_args = [C0, n0, jnp.broadcast_to(m0[..., None], m0.shape + (LANES,))]
        state_specs = [
            pl.BlockSpec((1, BB, heads, dk, dv), lambda b, c: (sl, b, 0, 0, 0)),
            pl.BlockSpec((1, BB, heads, dk), lambda b, c: (sl, b, 0, 0)),
            pl.BlockSpec((1, BB, heads, LANES), lambda b, c: (sl, b, 0, 0)),
        ]
    vmem = BB * (2 * (Ls * (2 * QW + 2 * VW + LANES) * 4 + Ls * VW * 2) + 4 * heads * dk * dv * 4) + 12 * MIB
    kern = functools.partial(_mlstm_kernel, heads=heads, dk=dk, dv=dv, chunk=L, nsub=Ls // L,
                             fresh=state is None)
    hm, C, n, m = pl.pallas_call(
        kern,
        grid=(B // BB, T // Ls),
        in_specs=[
            pl.BlockSpec((BB, Ls, QW), lambda b, c: (b, c, 0)),
            pl.BlockSpec((BB, Ls, QW), lambda b, c: (b, c, 1)),
            pl.BlockSpec((BB, Ls, VW), lambda b, c: (b, c, 1)),
            pl.BlockSpec((BB, Ls, VW), lambda b, c: (b, c, 2)),
            pl.BlockSpec((BB, Ls, LANES), lambda b, c: (b, c, 0)),
            pl.BlockSpec((1, LANES), lambda b, c: (0, 0)),
            pl.BlockSpec((1, VW), lambda b, c: (0, 0)),
        ] + state_specs,
        out_specs=[
            pl.BlockSpec((BB, Ls, VW), lambda b, c: (b, c, 0)),
            pl.BlockSpec((BB, heads, dk, dv), lambda b, c: (b, 0, 0, 0)),
            pl.BlockSpec((BB, heads, dk), lambda b, c: (b, 0, 0)),
            pl.BlockSpec((BB, heads, LANES), lambda b, c: (b, 0, 0)),
        ],
        out_shape=[
            jax.ShapeDtypeStruct((B, T, VW), BF16),
            jax.ShapeDtypeStruct((B, heads, dk, dv), F32),
            jax.ShapeDtypeStruct((B, heads, dk), F32),
            jax.ShapeDtypeStruct((B, heads, LANES), F32),
        ],
        compiler_params=_params(("parallel", "arbitrary"), vmem),
        name="mlstm",
    )(z, z, z, z, zg, bias, norm_g.reshape(1, VW), *state_args)
    return hm, C, n, m[:, :, 0]


def _hgrn_tables(nb):
    u = np.arange(nb)[:, None]
    j = np.arange(nb)[None, :]
    mats, masks = [j <= u], []
    w = nb // 2
    while w >= 1:
        r = (u // (2 * w)) * (2 * w) + w - 1
        upper = (u % (2 * w)) >= w
        mats.append(np.where(upper, (j > r) & (j <= u), (j > u) & (j <= r)))
        masks.append(((u // (2 * w)) == (j // (2 * w))) & upper & ((j % (2 * w)) < w))
        w //= 2
    mats.append(j > u)
    masks.append(u == j)
    masks = np.stack(masks).astype(np.float32)
    return np.concatenate(mats, 0).astype(np.float32), np.concatenate([masks, masks], axis=2)


def _hgrn_kernel(*refs, heads, dk, dv, layer, chunk, ncast, fresh):
    q_ref, f_ref, i_ref, g_ref, lb_ref, ng_ref, mst_ref, lvl_ref = refs[:8]
    n_in = 8 + (0 if fresh else 1) + ncast
    oh_ref, S_ref = refs[n_in:n_in + 2]
    st_scr = refs[-1]
    nb = HGRN_BLOCK
    nlev = lvl_ref.shape[0]
    c = pl.program_id(1)

    for src, dst in zip(refs[n_in - ncast:n_in], refs[n_in + 2:n_in + 2 + ncast]):
        dst[...] = src[...].astype(BF16)

    @pl.when(c == 0)
    def _():
        if fresh:
            st_scr[...] = jnp.zeros_like(st_scr)
        else:
            for h in range(heads):
                st_scr[h] = refs[8][0, 0, h].T

    lbs = lb_ref[...]
    e = jnp.exp(lbs - jnp.max(lbs, axis=0, keepdims=True))
    sm = e / jnp.sum(e, axis=0, keepdims=True)
    cum = sm[0:1, :]
    for j in range(1, layer + 1):
        cum = cum + sm[j:j + 1, :]
    lb = cum - sm[0:1, :]
    lb_floor = jnp.maximum(lb, LB_FLOOR)
    one_m = 1.0 - lb

    mst = mst_ref[...]
    masks = [lvl_ref[i] > 0.5 for i in range(nlev)]
    nt = (((1,), (1,)), ((), ()))
    first = lax.broadcasted_iota(jnp.int32, (nb, 2 * dk), 1) < dk

    def pair_diag(a):
        zero = jnp.zeros_like(a)
        return jnp.concatenate([jnp.where(first, a, zero), jnp.where(first, zero, a)], axis=0)

    def block(j, carry):
        rows = pl.ds(j * nb, nb)
        fx = f_ref[0, rows, :]
        en = jnp.exp(-jnp.abs(fx))
        rc = 1.0 / (1.0 + en)
        er = en * rc
        pos = fx >= 0.0
        g = jnp.log(lb_floor + one_m * jnp.where(pos, rc, er))
        kk = one_m * jnp.where(pos, er, rc)
        qs = _silu(q_ref[0, rows, :])
        vb = i_ref[0, rows, :].astype(BF16)
        gate = _silu(g_ref[0, rows, :])

        ghi = g.astype(BF16)
        gmid = (g - ghi.astype(F32)).astype(BF16)
        E = jnp.exp(jnp.dot(mst, ghi, preferred_element_type=F32)
                    + jnp.dot(mst, gmid, preferred_element_type=F32))
        EG = E[:nb]
        qG = (qs * EG).astype(BF16)
        kL = (kk * E[nlev * nb:]).astype(BF16)
        Ql = [(qs * E[(i + 1) * nb:(i + 2) * nb]).astype(BF16) for i in range(nlev - 1)] + [qs.astype(BF16)]
        Kl = [(kk * E[(i + 1) * nb:(i + 2) * nb]).astype(BF16) for i in range(nlev - 1)] + [kk.astype(BF16)]
        decay = EG[nb - 1:nb]

        pairs = [slice(2 * p * dk, (2 * p + 2) * dk) for p in range(heads // 2)]
        A2 = []
        for sl2 in pairs:
            a = jnp.zeros((nb, 2 * nb), F32)
            for i in range(nlev):
                a = jnp.where(masks[i], lax.dot_general(Ql[i][:, sl2], pair_diag(Kl[i][:, sl2]), nt,
                                                        preferred_element_type=F32), a)
            A2.append(a.astype(BF16))
        o_intra = jnp.concatenate(
            [jnp.dot(a, pair_diag(vb[:, sl2]), preferred_element_type=F32) for a, sl2 in zip(A2, pairs)], axis=1)

        for h in range(heads):
            sl = slice(h * dk, (h + 1) * dk)
            st = st_scr[h]
            o = lax.dot_general(qG[:, sl], st.astype(BF16), nt, preferred_element_type=F32)
            o = o + o_intra[:, sl]
            upd = lax.dot_general(vb[:, sl], kL[:, sl], (((0,), (0,)), ((), ())),
                                  preferred_element_type=F32)
            st_scr[h] = st * decay[:, sl] + upd

            ms = jnp.mean(o * o, axis=-1, keepdims=True)
            y = (o * lax.rsqrt(ms + EPS)) * ng_ref[:, sl] * gate[:, sl]
            oh_ref[0, rows, sl] = y.astype(BF16)
        return carry

    for j in range(chunk // nb):
        block(j, 0)

    @pl.when(c == pl.num_programs(1) - 1)
    def _():
        for h in range(heads):
            S_ref[0, h] = st_scr[h].T


def _hgrn_steps(B, T):
    Lc = HGRN_CHUNK if T % HGRN_CHUNK == 0 else T
    return Lc, B * (T // Lc)


def _hgrn_can_cast(B, T, weights):
    _, steps = _hgrn_steps(B, T)
    return all(w.shape[1] % (steps * BF16_ROWS) == 0 for w in weights)


def _hgrn(z, lower_bounds, layer, norm_g, dims, col_block, state=None, cast=()):
    B, T, _ = z.shape
    heads, dk, dv = dims
    HW = heads * dk
    assert dk == dv
    Lc, steps = _hgrn_steps(B, T)
    NC = T // Lc
    assert Lc % HGRN_BLOCK == 0
    depth = lower_bounds.shape[0]
    mst, lvl = _hgrn_tables(HGRN_BLOCK)
    mst = jnp.asarray(mst, BF16)
    lvl = jnp.asarray(lvl, F32)
    slabs = [(w.shape[1] // steps, w.shape[2]) for w in cast]
    vmem = 2 * (4 * Lc * HW * 4 + Lc * HW * 2) + 5 * heads * dk * dv * 4 \
        + 6 * mst.shape[0] * HW * 4 + sum(2 * rb * cols * 6 for rb, cols in slabs) + 16 * MIB
    kern = functools.partial(_hgrn_kernel, heads=heads, dk=dk, dv=dv, layer=layer, chunk=Lc, ncast=len(cast),
                             fresh=state is None)
    zspec = lambda idx: pl.BlockSpec((1, Lc, HW), lambda b, c: (b, c, idx))
    state_specs, state_args = [], []
    if state is not None:
        S0, sl = state
        state_args = [S0]
        state_specs = [pl.BlockSpec((1, 1, heads, dk, dv), lambda b, c: (sl, b, 0, 0, 0))]
    outs = pl.pallas_call(
        kern,
        grid=(B, NC),
        in_specs=[
            zspec(col_block), zspec(col_block + 1), zspec(col_block + 2), zspec(col_block + 3),
            pl.BlockSpec((depth, HW), lambda b, c: (0, 0)),
            pl.BlockSpec((1, HW), lambda b, c: (0, 0)),
            pl.BlockSpec(mst.shape, lambda b, c: (0, 0)),
            pl.BlockSpec(lvl.shape, lambda b, c: (0, 0, 0)),
        ] + state_specs
        + [pl.BlockSpec((None, rb, cols), lambda b, c: (layer, b * NC + c, 0)) for rb, cols in slabs],
        out_specs=[
            pl.BlockSpec((1, Lc, HW), lambda b, c: (b, c, 0)),
            pl.BlockSpec((1, heads, dk, dv), lambda b, c: (b, 0, 0, 0)),
        ] + [pl.BlockSpec((rb, cols), lambda b, c: (b * NC + c, 0)) for rb, cols in slabs],
        out_shape=[
            jax.ShapeDtypeStruct((B, T, HW), BF16),
            jax.ShapeDtypeStruct((B, heads, dk, dv), F32),
        ] + [jax.ShapeDtypeStruct(w.shape[1:], BF16) for w in cast],
        scratch_shapes=[pltpu.VMEM((heads, dv, dk), F32)],
        compiler_params=_params(("parallel", "arbitrary"), vmem),
        name="hgrn",
    )(z, z, z, z, lower_bounds, norm_g.reshape(1, HW), mst, lvl, *state_args, *cast)
    return outs[0], outs[1], list(outs[2:])


def _out_kernel(hm_ref, oh_ref, wa_ref, wb_ref, x_ref, ga_ref, xo_ref):
    acc = jnp.dot(hm_ref[...], wa_ref[...], preferred_element_type=F32)
    acc = acc + jnp.dot(oh_ref[...], wb_ref[...], preferred_element_type=F32)
    xo_ref[...] = x_ref[...] + ga_ref[...] * acc.reshape(xo_ref.shape)


def _out_proj(hm, oh, w_out, layer, x, mod, row0, *, tm=1024, tn=1024):
    B, T, D = x.shape
    MW, HW = hm.shape[-1], oh.shape[-1]
    assert MW == HW and w_out.shape[-2] == MW + HW
    G, R, TG, reps = _row_groups(B, T, tm)
    TM, M = TG * R, B * T
    tn = _col_tile(D, tn)
    vmem = 2 * (2 * TM * MW * 2 + 2 * MW * tn * 2 + 2 * TM * tn * 4 + tn * 4) + 2 * TM * tn * 4 + 4 * MIB
    return pl.pallas_call(
        _out_kernel,
        grid=(G // TG, D // tn),
        in_specs=[
            pl.BlockSpec((TM, MW), lambda i, j: (i, 0)),
            pl.BlockSpec((TM, HW), lambda i, j: (i, 0)),
            _weight_spec(w_out, layer, (MW, tn), lambda i, j: (0, j)),
            _weight_spec(w_out, layer, (HW, tn), lambda i, j: (1, j)),
            pl.BlockSpec((TG, R, tn), lambda i, j: (i, 0, j)),
            _mod_spec(mod, layer, row0, 2, TG, reps, tn, once=False),
        ],
        out_specs=pl.BlockSpec((TG, R, tn), lambda i, j: (i, 0, j)),
        out_shape=jax.ShapeDtypeStruct((G, R, D), F32),
        compiler_params=_params(("parallel", "arbitrary"), vmem),
        name="out_proj",
    )(hm.reshape(M, MW), oh.reshape(M, HW), w_out, w_out, x.reshape(G, R, D), mod).reshape(B, T, D)


def _ffn_kernel(x_ref, sc_ref, sh_ref, g_ref, ga_ref, wu_ref, wd_ref, fg_ref, o_ref, h_scr,
                *, final, n_split, f_split):
    f = pl.program_id(1)
    D = o_ref.shape[-1]
    dn = D // n_split
    ts = wu_ref.shape[-1] // f_split

    @pl.when(f == 0)
    def _():
        def piece(gs, rs, fs):
            h = _mod_norm(x_ref[gs, rs, :], g_ref[...], sc_ref[gs], sh_ref[gs])
            h_scr[fs, :] = h.reshape(-1, D).astype(BF16)
            o_ref[gs, rs, :] = jnp.zeros_like(h)

        _for_row_chunks(*x_ref.shape[:2], piece)

    blk = o_ref.shape[:-1] + (dn,)
    for k in range(f_split):
        u = jnp.dot(h_scr[...], wu_ref[:, k * ts:(k + 1) * ts], preferred_element_type=F32)
        a = jnp.square(jnp.maximum(u, 0.0)).astype(BF16)
        for n in range(n_split):
            sl = slice(n * dn, (n + 1) * dn)
            o_ref[:, :, sl] += jnp.dot(a, wd_ref[k * ts:(k + 1) * ts, sl], preferred_element_type=F32).reshape(blk)

    @pl.when(f == pl.num_programs(1) - 1)
    def _():
        def piece(gs, rs, fs):
            xo = x_ref[gs, rs, :] + ga_ref[gs] * o_ref[gs, rs, :]
            if final:
                ms = jnp.mean(xo * xo, axis=-1, keepdims=True)
                xo = (xo * lax.rsqrt(ms + EPS)) * fg_ref[...]
            o_ref[gs, rs, :] = xo

        _for_row_chunks(*x_ref.shape[:2], piece)


def _ffn(x, mod, row0, g, w_up, w_down, layer, final_g, *, final, tm=512, tf=1024):
    B, T, D = x.shape
    FF = w_up.shape[-1]
    G, R, TG, reps = _row_groups(B, T, tm)
    TM = TG * R
    tf = _col_tile(FF, tf)
    n_split = 4 if D % (4 * LANES) == 0 else 1
    f_split = 1
    vmem = 2 * TM * D * 4 + 2 * 2 * D * tf * 2 + (3 * TG + 4) * SUBLANES * D * 4 + TM * D * 2 + TM * tf * 6 \
        + 2 * TM * (D // n_split) * 4 + 4 * MIB
    kern = functools.partial(_ffn_kernel, final=final, n_split=n_split, f_split=f_split)
    once = pl.Buffered(1)
    vec = lambda part: _mod_spec(mod, layer, row0, part, TG, reps, D)
    return pl.pallas_call(
        kern,
        grid=(G // TG, FF // tf),
        in_specs=[
            pl.BlockSpec((TG, R, D), lambda i, f: (i, 0, 0), pipeline_mode=once),
            vec(4), vec(3),
            pl.BlockSpec((None, 1, D), lambda i, f: (layer, 0, 0), pipeline_mode=once),
            vec(5),
            _weight_spec(w_up, layer, (D, tf), lambda i, f: (0, f)),
            _weight_spec(w_down, layer, (tf, D), lambda i, f: (f, 0)),
            pl.BlockSpec((1, D), lambda i, f: (0, 0), pipeline_mode=once),
        ],
        out_specs=pl.BlockSpec((TG, R, D), lambda i, f: (i, 0, 0), pipeline_mode=once),
        out_shape=jax.ShapeDtypeStruct((G, R, D), F32),
        scratch_shapes=[pltpu.VMEM((TM, D), BF16)],
        compiler_params=_params(("parallel", "arbitrary"), vmem),
        name="ffn",
    )(x.reshape(G, R, D), mod, mod, g.reshape(-1, 1, D), mod, w_up, w_down,
      final_g.reshape(1, D)).reshape(B, T, D)


def _trunk(x, mod, row0, states, wts, late=None):
    depth = mod.shape[0]
    made = []
    Cs, ns, ms, Ss = [], [], [], []
    for l in range(depth):
        z, zg = _in_proj(x, mod, row0, wts["norm1_g"], wts["w_main"], wts["w_gate"], l)
        hm, C, n, m = _mlstm(z, zg, wts["b_gate"][l], wts["mlstm_norm_g"][l], wts["m_dims"],
                             state=None if states is None else (*states[:3], l))
        oh, S, cast = _hgrn(z, wts["lower_bounds"], l, wts["hgrn_norm_g"][l], wts["h_dims"], col_block=3,
                            state=None if states is None else (states[3], l),
                            cast=wts["late_f32"] if late is None else ())
        w_out, w_up, w_down = cast if late is None else late[l]
        made.append((w_out, w_up, w_down))
        x = _out_proj(hm, oh, w_out, l, x, mod, row0)
        x = _ffn(x, mod, row0, wts["norm2_g"], w_up, w_down, l, wts["final_g"], final=(l == depth - 1))
        Cs.append(C); ns.append(n); ms.append(m); Ss.append(S)
    return x, jnp.stack(Cs), jnp.stack(ns), jnp.stack(ms), jnp.stack(Ss), made


def kernel(x_prompt, x_sample, state_mlstm_C, state_mlstm_n, state_mlstm_m, state_hgrn_S, c_prompt, c_sample, w_mod, b_mod, norm1_g, w_in, b_gate, lower_bounds, mlstm_norm_g, hgrn_norm_g, w_out, norm2_g, w_up, w_down, final_g):
    depth = w_mod.shape[0]
    Bp, Bs = x_prompt.shape[0], x_sample.shape[0]
    _, _, heads, dk, dv = state_mlstm_C.shape
    _, _, hh, hdk, hdv = state_hgrn_S.shape
    gate0 = 2 * heads * dk + 2 * heads * dv
    ngate = 2 * heads

    w_main, w_gate = _prep_w_in(w_in, gate0, ngate)
    wts = dict(norm1_g=norm1_g, w_main=w_main, w_gate=w_gate, b_gate=b_gate, lower_bounds=lower_bounds,
               mlstm_norm_g=mlstm_norm_g, hgrn_norm_g=hgrn_norm_g, norm2_g=norm2_g, final_g=final_g,
               late_f32=(w_out, w_up, w_down), m_dims=(heads, dk, dv), h_dims=(hh, hdk, hdv))
    late = None
    if not _hgrn_can_cast(*x_prompt.shape[:2], wts["late_f32"]):
        late = [(w_out.astype(BF16), w_up.astype(BF16), w_down.astype(BF16))] * depth

    pad_rows = lambda c: jnp.pad(c, ((0, -c.shape[0] % BF16_ROWS), (0, 0)))
    c_all = jnp.concatenate([pad_rows(c_prompt), pad_rows(c_sample)], axis=0)
    row_s = Bp + (-Bp % BF16_ROWS)
    mod = _modulation(c_all, w_mod, b_mod)
    mod = mod.reshape(depth, c_all.shape[0], 1, mod.shape[-1])

    y_p, pC, pn, pm, pS, late = _trunk(x_prompt, mod, 0, None, wts, late)
    y_s, sC, sn, sm, sS, _ = _trunk(x_sample, mod, row_s,
                                    (state_mlstm_C, state_mlstm_n, state_mlstm_m, state_hgrn_S), wts, late)
    return (y_p, y_s, pC, pn, pm, pS, sC, sn, sm, sS)
```

```python
import functools

import jax
import jax.numpy as jnp
import numpy as np
from jax import lax
from jax.experimental import pallas as pl
from jax.experimental.pallas import tpu as pltpu

EPS = 1e-6
NEG_BIG = -1e30
LB_FLOOR = 1e-30
MLSTM_CHUNK = 256
MLSTM_STEP = 512
MLSTM_BATCH = 1
HGRN_BLOCK = 16
HGRN_CHUNK = 128
LANES = 128
SUBLANES = 8
BF16_ROWS = 16
MIB = 1024 * 1024
VMEM_CAP = 63 * MIB

F32 = jnp.float32
BF16 = jnp.bfloat16


def _params(semantics, vmem_bytes):
    return pltpu.CompilerParams(dimension_semantics=semantics,
                                vmem_limit_bytes=int(min(VMEM_CAP, vmem_bytes)))


def _sigmoid(x):
    return 1.0 / (1.0 + jnp.exp(-x))


def _silu(x):
    return x * _sigmoid(x)


def _log_sigmoid(x):
    return jnp.minimum(x, 0.0) - jnp.log1p(jnp.exp(-jnp.abs(x)))


def _lower_tri(n, dtype):
    r = lax.broadcasted_iota(jnp.int32, (n, n), 0)
    c = lax.broadcasted_iota(jnp.int32, (n, n), 1)
    return jnp.where(r >= c, 1.0, 0.0).astype(dtype)


def _cumsum_rows(x, tri):
    hi = x.astype(BF16)
    r1 = x - hi.astype(F32)
    mid = r1.astype(BF16)
    lo = (r1 - mid.astype(F32)).astype(BF16)
    dot = functools.partial(jnp.dot, preferred_element_type=F32)
    return (dot(tri, hi) + dot(tri, mid)) + dot(tri, lo)


def _col_tile(n, pref):
    return max(t for t in range(LANES, min(pref, n) + 1, LANES) if n % t == 0)


def _row_groups(B, T, tm_pref):
    if T >= tm_pref:
        assert T % tm_pref == 0
        R, TG = tm_pref, 1
    else:
        R = T
        TG = max(1, min(B, tm_pref // T))
        assert B % TG == 0
    return (B * T) // R, R, TG, T // R


def _for_row_chunks(TG, R, body, step=4 * BF16_ROWS):
    step = step if R % step == 0 else R
    per = R // step

    def piece(c, carry):
        t = c // per
        r = pl.multiple_of((c % per) * step, step)
        body(pl.ds(t, 1), pl.ds(r, step), pl.ds(pl.multiple_of(t * R + r, step), step))
        return carry

    lax.fori_loop(0, TG * per, piece, 0)


def _mod_spec(mod, layer, row0, part, TG, reps, width, once=True):
    D = mod.shape[-1] // 6
    per = D // width
    mode = dict(pipeline_mode=pl.Buffered(1)) if once else {}
    col = (lambda j: part) if per == 1 else (lambda j: part * per + j)
    if reps > 1:
        assert TG == 1
        return pl.BlockSpec((None, 1, 1, width), lambda i, j: (layer, row0 + i // reps, 0, col(j)), **mode)
    assert row0 % TG == 0
    return pl.BlockSpec((None, TG, 1, width), lambda i, j: (layer, row0 // TG + i, 0, col(j)), **mode)


def _weight_spec(w, layer, block, index):
    if w.ndim == 2:
        return pl.BlockSpec(block, index)
    return pl.BlockSpec((None,) + block, lambda i, j: (layer,) + index(i, j))


def _mod_norm(x, g, sc, sh):
    ms = jnp.mean(x * x, axis=-1, keepdims=True)
    return (x * lax.rsqrt(ms + EPS) * g) * (1.0 + sc) + sh


def _mod_kernel(c_ref, w_ref, b_ref, o_ref):
    c = c_ref[...]
    a = _silu(c).astype(BF16)
    w = w_ref[0].astype(BF16)
    o_ref[0] = jnp.dot(a, w, preferred_element_type=F32) + b_ref[0]


def _modulation(c_all, w_mod, b_mod, *, tn=512):
    depth, D, N = w_mod.shape
    MP = c_all.shape[0]
    tn = _col_tile(N, tn)
    vmem = 2 * (D * tn * 4 + MP * tn * 4 + tn * 4) + 2 * MP * D * 4 + D * tn * 2 + 4 * MIB
    return pl.pallas_call(
        _mod_kernel,
        grid=(depth, N // tn),
        in_specs=[
            pl.BlockSpec((MP, D), lambda l, j: (0, 0)),
            pl.BlockSpec((1, D, tn), lambda l, j: (l, 0, j)),
            pl.BlockSpec((1, 1, tn), lambda l, j: (l, 0, j)),
        ],
        out_specs=pl.BlockSpec((1, MP, tn), lambda l, j: (l, 0, j)),
        out_shape=jax.ShapeDtypeStruct((depth, MP, N), F32),
        compiler_params=_params(("parallel", "parallel"), vmem),
        name="modulation",
    )(c_all, w_mod, b_mod.reshape(depth, 1, N))


def _win_kernel(w_ref, o_ref):
    o_ref[...] = w_ref[0].T.astype(BF16)


def _prep_w_in(w_in, gate0, ngate, *, rows=512):
    depth, D, NC = w_in.shape
    NZ = NC - ngate
    wT = jnp.swapaxes(w_in, 1, 2)
    gate = jnp.pad(wT[:, gate0:gate0 + ngate, :], ((0, 0), (0, LANES - ngate), (0, 0))).astype(BF16)
    if (gate0 + ngate) % SUBLANES:
        return jnp.concatenate([w_in[:, :, :gate0], w_in[:, :, gate0 + ngate:]], axis=2).astype(BF16), gate
    rows = max(r for r in range(BF16_ROWS, rows + 1, BF16_ROWS) if gate0 % r == 0 and NZ % r == 0)
    src_row = lambda c: pl.multiple_of(jnp.where(c * rows < gate0, c * rows, c * rows + ngate), SUBLANES)
    main = pl.pallas_call(
        _win_kernel,
        grid=(depth, NZ // rows),
        in_specs=[pl.BlockSpec((pl.Element(1), pl.Element(rows), pl.Element(D)), lambda l, c: (l, src_row(c), 0))],
        out_specs=pl.BlockSpec((None, D, rows), lambda l, c: (l, 0, c)),
        out_shape=jax.ShapeDtypeStruct((depth, D, NZ), BF16),
        compiler_params=_params(("parallel", "parallel"), 2 * rows * D * 6 + rows * D * 4 + 4 * MIB),
        name="w_in_prep",
    )(wT)
    return main, gate


def _in_kernel(x_ref, sc_ref, sh_ref, g_ref, w_ref, wg_ref, z_ref, zg_ref, h_scr):
    nt = (((1,), (1,)), ((), ()))

    @pl.when(pl.program_id(1) == 0)
    def _():
        def piece(gs, rs, fs):
            h = _mod_norm(x_ref[gs, rs, :], g_ref[...], sc_ref[gs], sh_ref[gs])
            hb = h.reshape(-1, h.shape[-1]).astype(BF16)
            h_scr[fs, :] = hb
            zg_ref[fs, :] = lax.dot_general(hb, wg_ref[...], nt, preferred_element_type=F32)

        _for_row_chunks(*x_ref.shape[:2], piece)

    z_ref[...] = jnp.dot(h_scr[...], w_ref[...], preferred_element_type=F32)


def _in_proj(x, mod, row0, g, w_main, w_gate, layer, *, tm=512, tn=1024):
    B, T, D = x.shape
    NZ = w_main.shape[2]
    G, R, TG, reps = _row_groups(B, T, tm)
    TM, M = TG * R, B * T
    tn = _col_tile(NZ, tn)
    vmem = 2 * TM * D * 4 + D * LANES * 2 + (2 * TG + 1) * SUBLANES * D * 4 \
        + 2 * (D * tn * 2 + TM * tn * 4 + TM * LANES * 4) + TM * D * 2 + 8 * MIB
    once = pl.Buffered(1)
    z, zg = pl.pallas_call(
        _in_kernel,
        grid=(G // TG, NZ // tn),
        in_specs=[
            pl.BlockSpec((TG, R, D), lambda i, j: (i, 0, 0)),
            _mod_spec(mod, layer, row0, 1, TG, reps, D),
            _mod_spec(mod, layer, row0, 0, TG, reps, D),
            pl.BlockSpec((None, 1, D), lambda i, j: (layer, 0, 0), pipeline_mode=once),
            pl.BlockSpec((None, D, tn), lambda i, j: (layer, 0, j)),
            pl.BlockSpec((None, LANES, D), lambda i, j: (layer, 0, 0), pipeline_mode=once),
        ],
        out_specs=[
            pl.BlockSpec((TM, tn), lambda i, j: (i, j)),
            pl.BlockSpec((TM, LANES), lambda i, j: (i, 0)),
        ],
        out_shape=[jax.ShapeDtypeStruct((M, NZ), F32), jax.ShapeDtypeStruct((M, LANES), F32)],
        scratch_shapes=[pltpu.VMEM((TM, D), BF16)],
        compiler_params=_params(("parallel", "arbitrary"), vmem),
        name="in_proj",
    )(x.reshape(G, R, D), mod, mod, g.reshape(-1, 1, D), w_main, w_gate)
    return z.reshape(B, T, NZ), zg.reshape(B, T, LANES)


def _mlstm_kernel(*refs, heads, dk, dv, chunk, nsub, fresh):
    q_ref, k_ref, v_ref, o_ref, zg_ref, bg_ref, mg_ref = refs[:7]
    hm_ref, C_ref, n_ref, m_ref = refs[-4:]
    L = chunk

    @pl.when(pl.program_id(1) == 0)
    def _():
        if fresh:
            C_ref[...] = jnp.zeros_like(C_ref)
            n_ref[...] = jnp.zeros_like(n_ref)
            m_ref[...] = jnp.zeros_like(m_ref)
        else:
            C0_ref, n0_ref, m0_ref = refs[7:10]
            C_ref[...] = C0_ref[0]
            n_ref[...] = n0_ref[0]
            m_ref[...] = m0_ref[0]

    tri = _lower_tri(L, BF16)
    causal = (lax.broadcasted_iota(jnp.int32, (L, L), 0) >= lax.broadcasted_iota(jnp.int32, (L, L), 1))
    for s in range(nsub):
        for bb in range(q_ref.shape[0]):
            _mlstm_chunk(bb, slice(s * L, (s + 1) * L), tri, causal, q_ref, k_ref, v_ref, o_ref, zg_ref, bg_ref,
                         mg_ref, hm_ref, C_ref, n_ref, m_ref, heads=heads, dk=dk, dv=dv, L=L)


def _mlstm_chunk(bb, r, tri, causal, q_ref, k_ref, v_ref, o_ref, zg_ref, bg_ref, mg_ref,
                 hm_ref, C_ref, n_ref, m_ref, *, heads, dk, dv, L):
    gates = zg_ref[bb, r, :] + bg_ref[...]
    bsum = _cumsum_rows(_log_sigmoid(gates), tri)
    lane = lax.broadcasted_iota(jnp.int32, gates.shape, 1)
    rows = jnp.where(lane < heads, gates, bsum).T

    for h in range(heads):
        i_col = gates[:, h:h + 1]
        b_col = bsum[:, heads + h:heads + h + 1]
        i_row = rows[h:h + 1, :]
        b_row = rows[heads + h:heads + h + 1, :]
        m_prev = m_ref[bb, h:h + 1, 0:1]
        n_prev = n_ref[bb, h:h + 1, :]
        C_prev = C_ref[bb, h]

        q = q_ref[bb, r, h * dk:(h + 1) * dk] * (dk ** -0.5)
        k = k_ref[bb, r, h * dk:(h + 1) * dk]
        vb = v_ref[bb, r, h * dv:(h + 1) * dv].astype(BF16)
        qb = q.astype(BF16)

        D = jnp.where(causal, (b_col - b_row) + i_row, NEG_BIG)
        inter = b_col + m_prev
        mt = jnp.maximum(inter, jnp.max(D, axis=-1, keepdims=True))
        a_inter = jnp.exp(inter - mt)
        S = lax.dot_general(qb, k.astype(BF16), (((1,), (1,)), ((), ())),
                            preferred_element_type=F32) * jnp.exp(D - mt)
        num = a_inter * jnp.dot(qb, C_prev.astype(BF16), preferred_element_type=F32) \
            + jnp.dot(S.astype(BF16), vb, preferred_element_type=F32)
        den = a_inter * jnp.sum(q * n_prev, axis=-1, keepdims=True) + jnp.sum(S, axis=-1, keepdims=True)
        hh = num / jnp.maximum(jnp.abs(den), jnp.exp(-mt))

        b_last = b_col[L - 1:L, :]
        wlast = (b_last - b_col) + i_col
        m_new = jnp.maximum(b_last + m_prev, jnp.max(wlast, axis=0, keepdims=True))
        a_c = jnp.exp((b_last + m_prev) - m_new)
        kw = k * jnp.exp(wlast - m_new)
        C_ref[bb, h] = a_c * C_prev + lax.dot_general(kw.astype(BF16), vb, (((0,), (0,)), ((), ())),
                                                     preferred_element_type=F32)
        n_ref[bb, h:h + 1, :] = a_c * n_prev + jnp.sum(kw, axis=0, keepdims=True)
        m_ref[bb, h:h + 1, :] = jnp.broadcast_to(m_new, (1, LANES))

        ms = jnp.mean(hh * hh, axis=-1, keepdims=True)
        y = (hh * lax.rsqrt(ms + EPS)) * mg_ref[:, h * dv:(h + 1) * dv]
        y = y * _sigmoid(o_ref[bb, r, h * dv:(h + 1) * dv])
        hm_ref[bb, r, h * dv:(h + 1) * dv] = y.astype(BF16)


def _mlstm(z, zg, b_gate, norm_g, dims, state=None):
    B, T, _ = z.shape
    heads, dk, dv = dims
    QW, VW = heads * dk, heads * dv
    assert VW == 2 * QW and 2 * heads <= LANES
    L = MLSTM_CHUNK if T % MLSTM_CHUNK == 0 else T
    Ls = MLSTM_STEP if (T % MLSTM_STEP == 0 and MLSTM_STEP % L == 0) else L
    BB = MLSTM_BATCH if B % MLSTM_BATCH == 0 else 1
    bias = jnp.zeros((1, LANES), F32).at[0, :2 * heads].set(b_gate)
    state_specs, state_args = [], []
    if state is not None:
        C0, n0, m0, sl = state
        state_args = [C0, n0, jnp.broadcast_to(m0[..., None], m0.shape + (LANES,))]
        state_specs = [
            pl.BlockSpec((1, BB, heads, dk, dv), lambda b, c: (sl, b, 0, 0, 0)),
            pl.BlockSpec((1, BB, heads, dk), lambda b, c: (sl, b, 0, 0)),
            pl.BlockSpec((1, BB, heads, LANES), lambda b, c: (sl, b, 0, 0)),
        ]
    vmem = BB * (2 * (Ls * (2 * QW + 2 * VW + LANES) * 4 + Ls * VW * 2) + 4 * heads * dk * dv * 4) + 12 * MIB
    kern = functools.partial(_mlstm_kernel, heads=heads, dk=dk, dv=dv, chunk=L, nsub=Ls // L,
                             fresh=state is None)
    hm, C, n, m = pl.pallas_call(
        kern,
        grid=(B // BB, T // Ls),
        in_specs=[
            pl.BlockSpec((BB, Ls, QW), lambda b, c: (b, c, 0)),
            pl.BlockSpec((BB, Ls, QW), lambda b, c: (b, c, 1)),
            pl.BlockSpec((BB, Ls, VW), lambda b, c: (b, c, 1)),
            pl.BlockSpec((BB, Ls, VW), lambda b, c: (b, c, 2)),
            pl.BlockSpec((BB, Ls, LANES), lambda b, c: (b, c, 0)),
            pl.BlockSpec((1, LANES), lambda b, c: (0, 0)),
            pl.BlockSpec((1, VW), lambda b, c: (0, 0)),
        ] + state_specs,
        out_specs=[
            pl.BlockSpec((BB, Ls, VW), lambda b, c: (b, c, 0)),
            pl.BlockSpec((BB, heads, dk, dv), lambda b, c: (b, 0, 0, 0)),
            pl.BlockSpec((BB, heads, dk), lambda b, c: (b, 0, 0)),
            pl.BlockSpec((BB, heads, LANES), lambda b, c: (b, 0, 0)),
        ],
        out_shape=[
            jax.ShapeDtypeStruct((B, T, VW), BF16),
            jax.ShapeDtypeStruct((B, heads, dk, dv), F32),
            jax.ShapeDtypeStruct((B, heads, dk), F32),
            jax.ShapeDtypeStruct((B, heads, LANES), F32),
        ],
        compiler_params=_params(("parallel", "arbitrary"), vmem),
        name="mlstm",
    )(z, z, z, z, zg, bias, norm_g.reshape(1, VW), *state_args)
    return hm, C, n, m[:, :, 0]


def _hgrn_tables(nb):
    u = np.arange(nb)[:, None]
    j = np.arange(nb)[None, :]
    mats, masks = [j <= u], []
    w = nb // 2
    while w >= 1:
        r = (u // (2 * w)) * (2 * w) + w - 1
        upper = (u % (2 * w)) >= w
        mats.append(np.where(upper, (j > r) & (j <= u), (j > u) & (j <= r)))
        masks.append(((u // (2 * w)) == (j // (2 * w))) & upper & ((j % (2 * w)) < w))
        w //= 2
    mats.append(j > u)
    masks.append(u == j)
    masks = np.stack(masks).astype(np.float32)
    return np.concatenate(mats, 0).astype(np.float32), np.concatenate([masks, masks], axis=2)


def _hgrn_kernel(*refs, heads, dk, dv, layer, chunk, ncast, fresh):
    q_ref, f_ref, i_ref, g_ref, lb_ref, ng_ref, mst_ref, lvl_ref = refs[:8]
    n_in = 8 + (0 if fresh else 1) + ncast
    oh_ref, S_ref = refs[n_in:n_in + 2]
    st_scr = refs[-1]
    nb = HGRN_BLOCK
    nlev = lvl_ref.shape[0]
    c = pl.program_id(1)

    for src, dst in zip(refs[n_in - ncast:n_in], refs[n_in + 2:n_in + 2 + ncast]):
        dst[...] = src[...].astype(BF16)

    @pl.when(c == 0)
    def _():
        if fresh:
            st_scr[...] = jnp.zeros_like(st_scr)
        else:
            for h in range(heads):
                st_scr[h] = refs[8][0, 0, h].T

    lbs = lb_ref[...]
    e = jnp.exp(lbs - jnp.max(lbs, axis=0, keepdims=True))
    sm = e / jnp.sum(e, axis=0, keepdims=True)
    cum = sm[0:1, :]
    for j in range(1, layer + 1):
        cum = cum + sm[j:j + 1, :]
    lb = cum - sm[0:1, :]
    lb_floor = jnp.maximum(lb, LB_FLOOR)
    one_m = 1.0 - lb

    mst = mst_ref[...]
    masks = [lvl_ref[i] > 0.5 for i in range(nlev)]
    nt = (((1,), (1,)), ((), ()))
    first = lax.broadcasted_iota(jnp.int32, (nb, 2 * dk), 1) < dk

    def pair_diag(a):
        zero = jnp.zeros_like(a)
        return jnp.concatenate([jnp.where(first, a, zero), jnp.where(first, zero, a)], axis=0)

    def block(j, carry):
        rows = pl.ds(j * nb, nb)
        fx = f_ref[0, rows, :]
        en = jnp.exp(-jnp.abs(fx))
        rc = 1.0 / (1.0 + en)
        er = en * rc
        pos = fx >= 0.0
        g = jnp.log(lb_floor + one_m * jnp.where(pos, rc, er))
        kk = one_m * jnp.where(pos, er, rc)
        qs = _silu(q_ref[0, rows, :])
        vb = i_ref[0, rows, :].astype(BF16)
        gate = _silu(g_ref[0, rows, :])

        ghi = g.astype(BF16)
        gmid = (g - ghi.astype(F32)).astype(BF16)
        E = jnp.exp(jnp.dot(mst, ghi, preferred_element_type=F32)
                    + jnp.dot(mst, gmid, preferred_element_type=F32))
        EG = E[:nb]
        qG = (qs * EG).astype(BF16)
        kL = (kk * E[nlev * nb:]).astype(BF16)
        Ql = [(qs * E[(i + 1) * nb:(i + 2) * nb]).astype(BF16) for i in range(nlev - 1)] + [qs.astype(BF16)]
        Kl = [(kk * E[(i + 1) * nb:(i + 2) * nb]).astype(BF16) for i in range(nlev - 1)] + [kk.astype(BF16)]
        decay = EG[nb - 1:nb]

        pairs = [slice(2 * p * dk, (2 * p + 2) * dk) for p in range(heads // 2)]
        A2 = []
        for sl2 in pairs:
            a = jnp.zeros((nb, 2 * nb), F32)
            for i in range(nlev):
                a = jnp.where(masks[i], lax.dot_general(Ql[i][:, sl2], pair_diag(Kl[i][:, sl2]), nt,
                                                        preferred_element_type=F32), a)
            A2.append(a.astype(BF16))
        o_intra = jnp.concatenate(
            [jnp.dot(a, pair_diag(vb[:, sl2]), preferred_element_type=F32) for a, sl2 in zip(A2, pairs)], axis=1)

        for h in range(heads):
            sl = slice(h * dk, (h + 1) * dk)
            st = st_scr[h]
            o = lax.dot_general(qG[:, sl], st.astype(BF16), nt, preferred_element_type=F32)
            o = o + o_intra[:, sl]
            upd = lax.dot_general(vb[:, sl], kL[:, sl], (((0,), (0,)), ((), ())),
                                  preferred_element_type=F32)
            st_scr[h] = st * decay[:, sl] + upd

            ms = jnp.mean(o * o, axis=-1, keepdims=True)
            y = (o * lax.rsqrt(ms + EPS)) * ng_ref[:, sl] * gate[:, sl]
            oh_ref[0, rows, sl] = y.astype(BF16)
        return carry

    for j in range(chunk // nb):
        block(j, 0)

    @pl.when(c == pl.num_programs(1) - 1)
    def _():
        for h in range(heads):
            S_ref[0, h] = st_scr[h].T


def _hgrn_steps(B, T):
    Lc = HGRN_CHUNK if T % HGRN_CHUNK == 0 else T
    return Lc, B * (T // Lc)


def _hgrn_can_cast(B, T, weights):
    _, steps = _hgrn_steps(B, T)
    return all(w.shape[1] % (steps * BF16_ROWS) == 0 for w in weights)


def _hgrn(z, lower_bounds, layer, norm_g, dims, col_block, state=None, cast=()):
    B, T, _ = z.shape
    heads, dk, dv = dims
    HW = heads * dk
    assert dk == dv
    Lc, steps = _hgrn_steps(B, T)
    NC = T // Lc
    assert Lc % HGRN_BLOCK == 0
    depth = lower_bounds.shape[0]
    mst, lvl = _hgrn_tables(HGRN_BLOCK)
    mst = jnp.asarray(mst, BF16)
    lvl = jnp.asarray(lvl, F32)
    slabs = [(w.shape[1] // steps, w.shape[2]) for w in cast]
    vmem = 2 * (4 * Lc * HW * 4 + Lc * HW * 2) + 5 * heads * dk * dv * 4 \
        + 6 * mst.shape[0] * HW * 4 + sum(2 * rb * cols * 6 for rb, cols in slabs) + 16 * MIB
    kern = functools.partial(_hgrn_kernel, heads=heads, dk=dk, dv=dv, layer=layer, chunk=Lc, ncast=len(cast),
                             fresh=state is None)
    zspec = lambda idx: pl.BlockSpec((1, Lc, HW), lambda b, c: (b, c, idx))
    state_specs, state_args = [], []
    if state is not None:
        S0, sl = state
        state_args = [S0]
        state_specs = [pl.BlockSpec((1, 1, heads, dk, dv), lambda b, c: (sl, b, 0, 0, 0))]
    outs = pl.pallas_call(
        kern,
        grid=(B, NC),
        in_specs=[
            zspec(col_block), zspec(col_block + 1), zspec(col_block + 2), zspec(col_block + 3),
            pl.BlockSpec((depth, HW), lambda b, c: (0, 0)),
            pl.BlockSpec((1, HW), lambda b, c: (0, 0)),
            pl.BlockSpec(mst.shape, lambda b, c: (0, 0)),
            pl.BlockSpec(lvl.shape, lambda b, c: (0, 0, 0)),
        ] + state_specs
        + [pl.BlockSpec((None, rb, cols), lambda b, c: (layer, b * NC + c, 0)) for rb, cols in slabs],
        out_specs=[
            pl.BlockSpec((1, Lc, HW), lambda b, c: (b, c, 0)),
            pl.BlockSpec((1, heads, dk, dv), lambda b, c: (b, 0, 0, 0)),
        ] + [pl.BlockSpec((rb, cols), lambda b, c: (b * NC + c, 0)) for rb, cols in slabs],
        out_shape=[
            jax.ShapeDtypeStruct((B, T, HW), BF16),
            jax.ShapeDtypeStruct((B, heads, dk, dv), F32),
        ] + [jax.ShapeDtypeStruct(w.shape[1:], BF16) for w in cast],
        scratch_shapes=[pltpu.VMEM((heads, dv, dk), F32)],
        compiler_params=_params(("parallel", "arbitrary"), vmem),
        name="hgrn",
    )(z, z, z, z, lower_bounds, norm_g.reshape(1, HW), mst, lvl, *state_args, *cast)
    return outs[0], outs[1], list(outs[2:])


def _out_kernel(hm_ref, oh_ref, wa_ref, wb_ref, x_ref, ga_ref, xo_ref):
    acc = jnp.dot(hm_ref[...], wa_ref[...], preferred_element_type=F32)
    acc = acc + jnp.dot(oh_ref[...], wb_ref[...], preferred_element_type=F32)
    xo_ref[...] = x_ref[...] + ga_ref[...] * acc.reshape(xo_ref.shape)


def _out_proj(hm, oh, w_out, layer, x, mod, row0, *, tm=1024, tn=1024):
    B, T, D = x.shape
    MW, HW = hm.shape[-1], oh.shape[-1]
    assert MW == HW and w_out.shape[-2] == MW + HW
    G, R, TG, reps = _row_groups(B, T, tm)
    TM, M = TG * R, B * T
    tn = _col_tile(D, tn)
    vmem = 2 * (2 * TM * MW * 2 + 2 * MW * tn * 2 + 2 * TM * tn * 4 + tn * 4) + 2 * TM * tn * 4 + 4 * MIB
    return pl.pallas_call(
        _out_kernel,
        grid=(G // TG, D // tn),
        in_specs=[
            pl.BlockSpec((TM, MW), lambda i, j: (i, 0)),
            pl.BlockSpec((TM, HW), lambda i, j: (i, 0)),
            _weight_spec(w_out, layer, (MW, tn), lambda i, j: (0, j)),
            _weight_spec(w_out, layer, (HW, tn), lambda i, j: (1, j)),
            pl.BlockSpec((TG, R, tn), lambda i, j: (i, 0, j)),
            _mod_spec(mod, layer, row0, 2, TG, reps, tn, once=False),
        ],
        out_specs=pl.BlockSpec((TG, R, tn), lambda i, j: (i, 0, j)),
        out_shape=jax.ShapeDtypeStruct((G, R, D), F32),
        compiler_params=_params(("parallel", "arbitrary"), vmem),
        name="out_proj",
    )(hm.reshape(M, MW), oh.reshape(M, HW), w_out, w_out, x.reshape(G, R, D), mod).reshape(B, T, D)


def _ffn_kernel(x_ref, sc_ref, sh_ref, g_ref, ga_ref, wu_ref, wd_ref, fg_ref, o_ref, h_scr,
                *, final, n_split, f_split):
    f = pl.program_id(1)
    D = o_ref.shape[-1]
    dn = D // n_split
    ts = wu_ref.shape[-1] // f_split

    @pl.when(f == 0)
    def _():
        def piece(gs, rs, fs):
            h = _mod_norm(x_ref[gs, rs, :], g_ref[...], sc_ref[gs], sh_ref[gs])
            h_scr[fs, :] = h.reshape(-1, D).astype(BF16)
            o_ref[gs, rs, :] = jnp.zeros_like(h)

        _for_row_chunks(*x_ref.shape[:2], piece)

    blk = o_ref.shape[:-1] + (dn,)
    for k in range(f_split):
        u = jnp.dot(h_scr[...], wu_ref[:, k * ts:(k + 1) * ts], preferred_element_type=F32)
        a = jnp.square(jnp.maximum(u, 0.0)).astype(BF16)
        for n in range(n_split):
            sl = slice(n * dn, (n + 1) * dn)
            o_ref[:, :, sl] += jnp.dot(a, wd_ref[k * ts:(k + 1) * ts, sl], preferred_element_type=F32).reshape(blk)

    @pl.when(f == pl.num_programs(1) - 1)
    def _():
        def piece(gs, rs, fs):
            xo = x_ref[gs, rs, :] + ga_ref[gs] * o_ref[gs, rs, :]
            if final:
                ms = jnp.mean(xo * xo, axis=-1, keepdims=True)
                xo = (xo * lax.rsqrt(ms + EPS)) * fg_ref[...]
            o_ref[gs, rs, :] = xo

        _for_row_chunks(*x_ref.shape[:2], piece)


def _ffn(x, mod, row0, g, w_up, w_down, layer, final_g, *, final, tm=512, tf=1024):
    B, T, D = x.shape
    FF = w_up.shape[-1]
    G, R, TG, reps = _row_groups(B, T, tm)
    TM = TG * R
    tf = _col_tile(FF, tf)
    n_split = 4 if D % (4 * LANES) == 0 else 1
    f_split = 1
    vmem = 2 * TM * D * 4 + 2 * 2 * D * tf * 2 + (3 * TG + 4) * SUBLANES * D * 4 + TM * D * 2 + TM * tf * 6 \
        + 2 * TM * (D // n_split) * 4 + 4 * MIB
    kern = functools.partial(_ffn_kernel, final=final, n_split=n_split, f_split=f_split)
    once = pl.Buffered(1)
    vec = lambda part: _mod_spec(mod, layer, row0, part, TG, reps, D)
    return pl.pallas_call(
        kern,
        grid=(G // TG, FF // tf),
        in_specs=[
            pl.BlockSpec((TG, R, D), lambda i, f: (i, 0, 0), pipeline_mode=once),
            vec(4), vec(3),
            pl.BlockSpec((None, 1, D), lambda i, f: (layer, 0, 0), pipeline_mode=once),
            vec(5),
            _weight_spec(w_up, layer, (D, tf), lambda i, f: (0, f)),
            _weight_spec(w_down, layer, (tf, D), lambda i, f: (f, 0)),
            pl.BlockSpec((1, D), lambda i, f: (0, 0), pipeline_mode=once),
        ],
        out_specs=pl.BlockSpec((TG, R, D), lambda i, f: (i, 0, 0), pipeline_mode=once),
        out_shape=jax.ShapeDtypeStruct((G, R, D), F32),
        scratch_shapes=[pltpu.VMEM((TM, D), BF16)],
        compiler_params=_params(("parallel", "arbitrary"), vmem),
        name="ffn",
    )(x.reshape(G, R, D), mod, mod, g.reshape(-1, 1, D), mod, w_up, w_down,
      final_g.reshape(1, D)).reshape(B, T, D)


def _trunk(x, mod, row0, states, wts, late=None):
    depth = mod.shape[0]
    made = []
    Cs, ns, ms, Ss = [], [], [], []
    for l in range(depth):
        z, zg = _in_proj(x, mod, row0, wts["norm1_g"], wts["w_main"], wts["w_gate"], l)
        hm, C, n, m = _mlstm(z, zg, wts["b_gate"][l], wts["mlstm_norm_g"][l], wts["m_dims"],
                             state=None if states is None else (*states[:3], l))
        oh, S, cast = _hgrn(z, wts["lower_bounds"], l, wts["hgrn_norm_g"][l], wts["h_dims"], col_block=3,
                            state=None if states is None else (states[3], l),
                            cast=wts["late_f32"] if late is None else ())
        w_out, w_up, w_down = cast if late is None else late[l]
        made.append((w_out, w_up, w_down))
        x = _out_proj(hm, oh, w_out, l, x, mod, row0)
        x = _ffn(x, mod, row0, wts["norm2_g"], w_up, w_down, l, wts["final_g"], final=(l == depth - 1))
        Cs.append(C); ns.append(n); ms.append(m); Ss.append(S)
    return x, jnp.stack(Cs), jnp.stack(ns), jnp.stack(ms), jnp.stack(Ss), made


def kernel(x_prompt, x_sample, state_mlstm_C, state_mlstm_n, state_mlstm_m, state_hgrn_S, c_prompt, c_sample, w_mod, b_mod, norm1_g, w_in, b_gate, lower_bounds, mlstm_norm_g, hgrn_norm_g, w_out, norm2_g, w_up, w_down, final_g):
    depth = w_mod.shape[0]
    Bp, Bs = x_prompt.shape[0], x_sample.shape[0]
    _, _, heads, dk, dv = state_mlstm_C.shape
    _, _, hh, hdk, hdv = state_hgrn_S.shape
    gate0 = 2 * heads * dk + 2 * heads * dv
    ngate = 2 * heads

    w_main, w_gate = _prep_w_in(w_in, gate0, ngate)
    wts = dict(norm1_g=norm1_g, w_main=w_main, w_gate=w_gate, b_gate=b_gate, lower_bounds=lower_bounds,
               mlstm_norm_g=mlstm_norm_g, hgrn_norm_g=hgrn_norm_g, norm2_g=norm2_g, final_g=final_g,
               late_f32=(w_out, w_up, w_down), m_dims=(heads, dk, dv), h_dims=(hh, hdk, hdv))
    late = None
    if not _hgrn_can_cast(*x_prompt.shape[:2], wts["late_f32"]):
        late = [(w_out.astype(BF16), w_up.astype(BF16), w_down.astype(BF16))] * depth

    pad_rows = lambda c: jnp.pad(c, ((0, -c.shape[0] % BF16_ROWS), (0, 0)))
    c_all = jnp.concatenate([pad_rows(c_prompt), pad_rows(c_sample)], axis=0)
    row_s = Bp + (-Bp % BF16_ROWS)
    mod = _modulation(c_all, w_mod, b_mod)
    mod = mod.reshape(depth, c_all.shape[0], 1, mod.shape[-1])

    y_p, pC, pn, pm, pS, late = _trunk(x_prompt, mod, 0, None, wts, late)
    y_s, sC, sn, sm, sS, _ = _trunk(x_sample, mod, row_s,
                                    (state_mlstm_C, state_mlstm_n, state_mlstm_m, state_hgrn_S), wts, late)
    return (y_p, y_s, pC, pn, pm, pS, sC, sn, sm, sS)
```
